```python
import math
import jax, jax.numpy as jnp
from jax import lax
import numpy as np

D_MODEL = 1024
BATCH = 4
SEQ = 8192
DEPTH = 4

CONV_CH = D_MODEL // 2
CONV_WIDTH = 31
SGU_CH = D_MODEL // 2
SGU_GROUPS = 8
SGU_GROUP_CH = SGU_CH // SGU_GROUPS
SGU_CHUNK = 128
EVEN_IN = 2 * CONV_CH + 2 * SGU_CH
ATTN_GROUPS = ((128, 1), (512, 4), (2048, 16))
N_ATTN_GROUPS = len(ATTN_GROUPS)
HEADS_PER_GROUP = 8
HEAD_DIM = 128
ATTN_WIDTH = HEADS_PER_GROUP * HEAD_DIM
ODD_IN = N_ATTN_GROUPS * 3 * ATTN_WIDTH
ATTN_BLOCK = 128
ROT_DIM = HEAD_DIM // 4
ROPE_THETA = 500000.0
FFN_HIDDEN = ((-(-8 * D_MODEL // 3) + 255) // 256) * 256
N_EVEN = (DEPTH + 1) // 2
N_ODD = DEPTH // 2
DN_ALPHA = (2.0 * DEPTH) ** 0.25
DN_BETA = (8.0 * DEPTH) ** -0.25
LN_EPS = 1e-5

kernel_name = 'hybrid_conv_sgu_dilated_attn_deepnorm'


def layer_norm(x, g, b):
    xf = x.astype(jnp.float32)
    mu = jnp.mean(xf, axis=-1, keepdims=True)
    xc = xf - mu
    var = jnp.mean(xc * xc, axis=-1, keepdims=True)
    y = xc * lax.rsqrt(var + LN_EPS)
    return (y * g.astype(jnp.float32) + b.astype(jnp.float32)).astype(x.dtype)


def partial_rope(t, positions):
    half = ROT_DIM // 2
    inv_freq = ROPE_THETA ** (-jnp.arange(0, ROT_DIM, 2, dtype=jnp.float32) / ROT_DIM)
    ang = positions.astype(jnp.float32)[..., None] * inv_freq
    cos = jnp.cos(ang)[:, :, None, :]
    sin = jnp.sin(ang)[:, :, None, :]
    tf = t.astype(jnp.float32)
    r1, r2, rest = tf[..., :half], tf[..., half:ROT_DIM], tf[..., ROT_DIM:]
    out = jnp.concatenate([r1 * cos - r2 * sin, r2 * cos + r1 * sin, rest], axis=-1)
    return out.astype(t.dtype)


def dilated_window_attention(q, k, v, dilation, steps):
    B, S, H, Dh = q.shape
    Q = ATTN_BLOCK
    L = -(-S // dilation)
    L = -(-L // Q) * Q
    nb = L // Q
    S_pad = L * dilation
    pad = ((0, 0), (0, S_pad - S), (0, 0), (0, 0))

    def to_streams(t):
        t = jnp.pad(t, pad).reshape(B, L, dilation, H, Dh)
        return t.transpose(0, 2, 3, 1, 4).reshape(B, dilation, H, nb, Q, Dh)

    def with_prev(t):
        prev = jnp.pad(t, ((0, 0), (0, 0), (0, 0), (1, 0), (0, 0), (0, 0)))[:, :, :, :-1]
        return jnp.concatenate([prev, t], axis=4)

    qs = to_streams(q)
    kb = with_prev(to_streams(k))
    vb = with_prev(to_streams(v))
    s = jnp.einsum('brhnqd,brhnkd->brhnqk', qs, kb,
                   preferred_element_type=jnp.float32) * (Dh ** -0.5)
    qi = jnp.arange(Q)[:, None]
    kj = jnp.arange(2 * Q)[None, :]
    dist = Q + qi - kj
    band = (dist >= 0) & (dist <= steps)
    has_prev = (jnp.arange(nb)[:, None, None] > 0) | (kj[None] >= Q)
    valid = band[None] & has_prev
    s = jnp.where(valid, s, -jnp.inf)
    m = jnp.max(s, axis=-1, keepdims=True)
    p = jnp.exp(s - m)
    denom = jnp.sum(p, axis=-1, keepdims=True)
    o = jnp.einsum('brhnqk,brhnkd->brhnqd', p, vb.astype(jnp.float32)) / denom
    lse = (m + jnp.log(denom))[..., 0]
    o = o.reshape(B, dilation, H, L, Dh).transpose(0, 3, 1, 2, 4).reshape(B, S_pad, H, Dh)[:, :S]
    lse = lse.reshape(B, dilation, H, L).transpose(0, 3, 1, 2).reshape(B, S_pad, H)[:, :S]
    return o, lse


def conv_sgu_mixer(x, w_in, conv_w, conv_b, conv_ln_g, conv_ln_b,
                   sgu_ln_g, sgu_ln_b, w_spatial, b_spatial, w_out):
    B, S, _ = x.shape
    h = x @ w_in
    a_val = h[..., :CONV_CH]
    a_gate = h[..., CONV_CH:2 * CONV_CH]
    z = h[..., 2 * CONV_CH:]
    a = a_val * jax.nn.sigmoid(a_gate)
    a = lax.conv_general_dilated(a, conv_w[:, None, :].astype(a.dtype), window_strides=(1,),
                                 padding=[(CONV_WIDTH - 1, 0)],
                                 dimension_numbers=('NWC', 'WIO', 'NWC'),
                                 feature_group_count=CONV_CH) + conv_b
    a = jax.nn.silu(layer_norm(a, conv_ln_g, conv_ln_b))
    z = jax.nn.gelu(z, approximate=False)
    u, v = z[..., :SGU_CH], z[..., SGU_CH:]
    v = layer_norm(v, sgu_ln_g, sgu_ln_b).reshape(B, S // SGU_CHUNK, SGU_CHUNK, SGU_GROUPS, SGU_GROUP_CH)
    causal = jnp.tril(jnp.ones((SGU_CHUNK, SGU_CHUNK), dtype=w_spatial.dtype))
    sv = jnp.einsum('hts,bnshc->bnthc', w_spatial * causal, v) + b_spatial.T[None, None, :, :, None]
    g = u * sv.reshape(B, S, SGU_CH)
    return jnp.concatenate([a, g], axis=-1) @ w_out


def dilated_attention_mixer(x, positions, w_qkv, w_out):
    B, S, _ = x.shape
    qkv = (x @ w_qkv).reshape(B, S, N_ATTN_GROUPS, 3, HEADS_PER_GROUP, HEAD_DIM)
    outs, lses = [], []
    for gi, (window, dilation) in enumerate(ATTN_GROUPS):
        q = partial_rope(qkv[:, :, gi, 0], positions)
        k = partial_rope(qkv[:, :, gi, 1], positions)
        v = qkv[:, :, gi, 2]
        o, lse = dilated_window_attention(q, k, v, dilation, window // dilation)
        outs.append(o)
        lses.append(lse)
    wts = jax.nn.softmax(jnp.stack(lses, axis=0), axis=0)
    o = jnp.sum(wts[..., None] * jnp.stack(outs, axis=0), axis=0)
    return o.reshape(B, S, ATTN_WIDTH).astype(x.dtype) @ w_out


def swiglu(x, w_gate, w_up, w_down):
    return (jax.nn.silu(x @ w_gate) * (x @ w_up)) @ w_down


def setup_inputs(seed: int = 0) -> dict:
    key = jax.random.key(seed)
    ks = jax.random.split(key, 20)
    f32 = jnp.float32

    def nrm(k, shape, scale):
        return jax.random.normal(k, shape, f32) * scale

    x = jax.random.normal(ks[0], (BATCH, SEQ, D_MODEL), f32)
    positions = jnp.broadcast_to(jnp.arange(SEQ, dtype=jnp.int32), (BATCH, SEQ))
    ev_w_in = nrm(ks[1], (N_EVEN, D_MODEL, EVEN_IN), D_MODEL ** -0.5)
    ev_conv_w = nrm(ks[2], (N_EVEN, CONV_WIDTH, CONV_CH), CONV_WIDTH ** -0.5)
    ev_conv_b = nrm(ks[3], (N_EVEN, CONV_CH), 0.02)
    ev_conv_ln_g = 1.0 + nrm(ks[4], (N_EVEN, CONV_CH), 0.05)
    ev_conv_ln_b = nrm(ks[5], (N_EVEN, CONV_CH), 0.02)
    ev_sgu_ln_g = 1.0 + nrm(ks[6], (N_EVEN, SGU_CH), 0.05)
    ev_sgu_ln_b = nrm(ks[7], (N_EVEN, SGU_CH), 0.02)
    ev_w_spatial = nrm(ks[8], (N_EVEN, SGU_GROUPS, SGU_CHUNK, SGU_CHUNK), SGU_CHUNK ** -0.5)
    ev_b_spatial = 1.0 + nrm(ks[9], (N_EVEN, SGU_GROUPS, SGU_CHUNK), 0.1)
    ev_w_out = nrm(ks[10], (N_EVEN, CONV_CH + SGU_CH, D_MODEL), DN_BETA * (CONV_CH + SGU_CH) ** -0.5)
    od_w_qkv = nrm(ks[11], (N_ODD, D_MODEL, ODD_IN), D_MODEL ** -0.5)
    od_w_out = nrm(ks[12], (N_ODD, ATTN_WIDTH, D_MODEL), DN_BETA * ATTN_WIDTH ** -0.5)
    ffn_w_gate = nrm(ks[13], (DEPTH, D_MODEL, FFN_HIDDEN), D_MODEL ** -0.5)
    ffn_w_up = nrm(ks[14], (DEPTH, D_MODEL, FFN_HIDDEN), D_MODEL ** -0.5)
    ffn_w_down = nrm(ks[15], (DEPTH, FFN_HIDDEN, D_MODEL), DN_BETA * FFN_HIDDEN ** -0.5)
    ln_g = 1.0 + nrm(ks[16], (DEPTH, 2, D_MODEL), 0.05)
    ln_b = nrm(ks[17], (DEPTH, 2, D_MODEL), 0.02)
    return {'x': x, 'positions': positions,
            'ev_w_in': ev_w_in, 'ev_conv_w': ev_conv_w, 'ev_conv_b': ev_conv_b,
            'ev_conv_ln_g': ev_conv_ln_g, 'ev_conv_ln_b': ev_conv_ln_b,
            'ev_sgu_ln_g': ev_sgu_ln_g, 'ev_sgu_ln_b': ev_sgu_ln_b,
            'ev_w_spatial': ev_w_spatial, 'ev_b_spatial': ev_b_spatial, 'ev_w_out': ev_w_out,
            'od_w_qkv': od_w_qkv, 'od_w_out': od_w_out,
            'ffn_w_gate': ffn_w_gate, 'ffn_w_up': ffn_w_up, 'ffn_w_down': ffn_w_down,
            'ln_g': ln_g, 'ln_b': ln_b}


def reference(x, positions, ev_w_in, ev_conv_w, ev_conv_b, ev_conv_ln_g, ev_conv_ln_b,
              ev_sgu_ln_g, ev_sgu_ln_b, ev_w_spatial, ev_b_spatial, ev_w_out,
              od_w_qkv, od_w_out, ffn_w_gate, ffn_w_up, ffn_w_down, ln_g, ln_b):
    for layer in range(DEPTH):
        i = layer // 2
        if layer % 2 == 0:
            h = conv_sgu_mixer(x, ev_w_in[i], ev_conv_w[i], ev_conv_b[i], ev_conv_ln_g[i], ev_conv_ln_b[i],
                               ev_sgu_ln_g[i], ev_sgu_ln_b[i], ev_w_spatial[i], ev_b_spatial[i], ev_w_out[i])
        else:
            h = dilated_attention_mixer(x, positions, od_w_qkv[i], od_w_out[i])
        x = layer_norm(DN_ALPHA * x + h, ln_g[layer, 0], ln_b[layer, 0])
        x = layer_norm(DN_ALPHA * x + swiglu(x, ffn_w_gate[layer], ffn_w_up[layer], ffn_w_down[layer]),
                       ln_g[layer, 1], ln_b[layer, 1])
    return x
```

```python
import functools

import jax
import jax.numpy as jnp
from jax import lax
from jax.experimental import pallas as pl
from jax.experimental.pallas import tpu as pltpu

F32 = jnp.float32
BF16 = jnp.bfloat16

D_MODEL = 1024
DEPTH = 4
CONV_CH = 512
CONV_WIDTH = 31
SGU_CH = 512
SGU_GROUPS = 8
SGU_CHUNK = 128
HEADS = 8
HEAD_DIM = 128
ATTN_BLOCK = 128
ATTN_DILATIONS = (1, 4, 16)
ROT_DIM = 32
ROPE_THETA = 500000.0
FFN_HIDDEN = 2816
DN_ALPHA = (2.0 * DEPTH) ** 0.25
LN_EPS = 1e-5
NEG_BIG = -1e30

LANES = 128
VMEM_LIMIT_BYTES = 56 * 1024 * 1024

TM_BACK = 512
TM_EVEN = 512
ATTN_TILE = 2048
N_SLABS = D_MODEL // LANES
QKV_ROWS = 512
CONV_HALO = 32


def _layer_norm(v, g, b):
    mu = jnp.mean(v, axis=-1, keepdims=True)
    c = v - mu
    var = jnp.mean(c * c, axis=-1, keepdims=True)
    return c * lax.rsqrt(var + LN_EPS) * g + b


def _resident(shape):
    nd = len(shape)
    return pl.BlockSpec(shape, lambda *_: (0,) * nd, pipeline_mode=pl.Buffered(1))


def _back_kernel(x_ref, c_ref, wo_ref, wg_ref, wu_ref, wd_ref, ln_ref, o_ref):
    x = x_ref[...]
    h = jnp.dot(c_ref[...], wo_ref[...], preferred_element_type=F32)
    x1 = _layer_norm(DN_ALPHA * x + h, ln_ref[0:1, :], ln_ref[1:2, :])
    x1b = x1.astype(BF16)
    g = jnp.dot(x1b, wg_ref[...], preferred_element_type=F32)
    u = jnp.dot(x1b, wu_ref[...], preferred_element_type=F32)
    a = (g * jax.nn.sigmoid(g) * u).astype(BF16)
    y = jnp.dot(a, wd_ref[...], preferred_element_type=F32)
    o_ref[...] = _layer_norm(DN_ALPHA * x1 + y, ln_ref[2:3, :], ln_ref[3:4, :])


def _back(x2d, c2d, w_out, w_gate, w_up, w_down, ln4):
    n = x2d.shape[0]
    tile = lambda: pl.BlockSpec((TM_BACK, D_MODEL), lambda i: (i, 0))
    return pl.pallas_call(
        _back_kernel,
        grid=(n // TM_BACK,),
        in_specs=[tile(), tile(),
                  _resident(w_out.shape), _resident(w_gate.shape),
                  _resident(w_up.shape), _resident(w_down.shape),
                  _resident(ln4.shape)],
        out_specs=tile(),
        out_shape=jax.ShapeDtypeStruct((n, D_MODEL), F32),
        compiler_params=pltpu.CompilerParams(
            dimension_semantics=("arbitrary",), vmem_limit_bytes=VMEM_LIMIT_BYTES),
        name="back",
    )(x2d, c2d, w_out, w_gate, w_up, w_down, ln4)


def _even_kernel(x_ref, win_ref, cw_ref, cvec_ref, svec_ref, wsp_ref, bsp_ref, o_ref, abuf_ref):
    tm = TM_EVEN
    h = jnp.dot(x_ref[0].astype(BF16), win_ref[...], preferred_element_type=F32)

    a = h[:, :CONV_CH] * jax.nn.sigmoid(h[:, CONV_CH:2 * CONV_CH])

    @pl.when(pl.program_id(1) == 0)
    def _():
        abuf_ref[0:CONV_HALO, :] = jnp.zeros((CONV_HALO, CONV_CH), F32)

    abuf_ref[CONV_HALO:CONV_HALO + tm, :] = a
    base = CONV_HALO - (CONV_WIDTH - 1)
    conv = jnp.broadcast_to(cvec_ref[0:1, :], (tm, CONV_CH))
    for j in range(CONV_WIDTH):
        conv = conv + cw_ref[j:j + 1, :] * abuf_ref[base + j:base + j + tm, :]
    abuf_ref[0:CONV_HALO, :] = abuf_ref[tm:tm + CONV_HALO, :]
    an = _layer_norm(conv, cvec_ref[1:2, :], cvec_ref[2:3, :])
    o_ref[0, :, 0:CONV_CH] = (an * jax.nn.sigmoid(an)).astype(BF16)

    z = h[:, 2 * CONV_CH:]
    z = 0.5 * z * (1.0 + lax.erf(z * (2.0 ** -0.5)))
    u = z[:, :SGU_CH]
    v = _layer_norm(z[:, SGU_CH:], svec_ref[0:1, :], svec_ref[1:2, :])
    row = lax.broadcasted_iota(jnp.int32, (SGU_CHUNK, SGU_CHUNK), 0)
    col = lax.broadcasted_iota(jnp.int32, (SGU_CHUNK, SGU_CHUNK), 1)
    tril = col <= row
    lane = lax.broadcasted_iota(jnp.int32, (tm, LANES), 1)
    low_half = lane < (LANES // 2)
    for p in range(SGU_GROUPS // 2):
        w_pair = jnp.concatenate(
            [jnp.where(tril, wsp_ref[2 * p], 0.0), jnp.where(tril, wsp_ref[2 * p + 1], 0.0)],
            axis=1).astype(BF16)
        v_slab = v[:, p * LANES:(p + 1) * LANES]
        v_lo = jnp.where(low_half, v_slab, 0.0).astype(BF16)
        v_hi = jnp.where(low_half, 0.0, v_slab).astype(BF16)
        for c in range(tm // SGU_CHUNK):
            rows = slice(c * SGU_CHUNK, (c + 1) * SGU_CHUNK)
            rhs = jnp.concatenate([v_lo[rows], v_hi[rows]], axis=0)
            sv = jnp.dot(w_pair, rhs, preferred_element_type=F32) + bsp_ref[:, p * LANES:(p + 1) * LANES]
            o_ref[0, rows, CONV_CH + p * LANES:CONV_CH + (p + 1) * LANES] = (
                u[rows, p * LANES:(p + 1) * LANES] * sv).astype(BF16)


def _even_front(x, w_in, conv_w, cvec, svec, w_spatial, bias_full):
    b, s, _ = x.shape
    return pl.pallas_call(
        _even_kernel,
        grid=(b, s // TM_EVEN),
        in_specs=[pl.BlockSpec((1, TM_EVEN, D_MODEL), lambda bi, i: (bi, i, 0)),
                  _resident(w_in.shape), _resident(conv_w.shape), _resident(cvec.shape),
                  _resident(svec.shape), _resident(w_spatial.shape), _resident(bias_full.shape)],
        out_specs=pl.BlockSpec((1, TM_EVEN, D_MODEL), lambda bi, i: (bi, i, 0)),
        out_shape=jax.ShapeDtypeStruct((b, s, D_MODEL), BF16),
        scratch_shapes=[pltpu.VMEM((CONV_HALO + TM_EVEN, CONV_CH), F32)],
        compiler_params=pltpu.CompilerParams(
            dimension_semantics=("arbitrary", "arbitrary"), vmem_limit_bytes=VMEM_LIMIT_BYTES),
        name="even_front",
    )(x, w_in, conv_w, cvec, svec, w_spatial, bias_full)


def _stream_blocks(dilation):
    if dilation == 1:
        return [(n * ATTN_BLOCK, 1) for n in range(ATTN_TILE // ATTN_BLOCK)]
    if dilation == 4:
        return [(n * ATTN_BLOCK * 4 + rho, 4) for n in range(4) for rho in range(4)]
    return [(rho, 16) for rho in range(16)]


def _qkv_kernel(*refs):
    x_slabs = refs[:N_SLABS]
    pos_ref, freq_ref, w_ref, o_ref, xp_ref, cos_ref, sin_ref = refs[N_SLABS:]
    c = pl.program_id(2)

    @pl.when(c == 0)
    def _():
        ang = pos_ref[0].astype(F32) * freq_ref[...]
        cos_ref[0] = jnp.cos(ang)
        sin_ref[0] = jnp.sin(ang)
        for gi, dilation in enumerate(ATTN_DILATIONS):
            for blk, (start, stride) in enumerate(_stream_blocks(dilation)):
                rows = pl.ds(start, ATTN_BLOCK, stride=stride) if stride > 1 else pl.ds(start, ATTN_BLOCK)
                dst = slice(blk * ATTN_BLOCK, (blk + 1) * ATTN_BLOCK)
                for sl in range(N_SLABS):
                    xp_ref[gi, dst, sl * LANES:(sl + 1) * LANES] = x_slabs[sl][0, rows, :].astype(BF16)
                if gi > 0:
                    cos_ref[gi, dst, :] = cos_ref[0, rows, :]
                    sin_ref[gi, dst, :] = sin_ref[0, rows, :]

    gi = c // 3
    t = c % 3
    scale = jnp.where(t == 0, HEAD_DIM ** -0.5, 1.0).astype(F32)
    rotate = t < 2
    lane = lax.broadcasted_iota(jnp.int32, (QKV_ROWS, LANES), 1)
    half = ROT_DIM // 2
    for rc in range(ATTN_TILE // QKV_ROWS):
        rows = slice(rc * QKV_ROWS, (rc + 1) * QKV_ROWS)
        y = jnp.dot(xp_ref[gi, rows, :], w_ref[...], preferred_element_type=F32)
        cos = jnp.where(rotate, cos_ref[gi, rows, :], 1.0) * scale
        sin = jnp.where(rotate, sin_ref[gi, rows, :], 0.0) * scale
        sin_up = jnp.where((lane >= half) & (lane < ROT_DIM), sin, 0.0)
        sin_dn = jnp.where(lane < half, -sin, 0.0)
        for hd in range(HEADS):
            yh = y[:, hd * HEAD_DIM:(hd + 1) * HEAD_DIM]
            rot = (yh * cos + pltpu.roll(yh, half, axis=1) * sin_up
                   + pltpu.roll(yh, LANES - half, axis=1) * sin_dn)
            o_ref[0, 0, hd, rows, :] = rot.astype(BF16)


def _qkv_proj(x, pos3, freq, w_qkv):
    b, s, _ = x.shape
    n_chunks = w_qkv.shape[1] // D_MODEL
    slab = lambda sl: pl.BlockSpec((1, ATTN_TILE, LANES), lambda bi, ti, c: (bi, ti, sl))
    return pl.pallas_call(
        _qkv_kernel,
        grid=(b, s // ATTN_TILE, n_chunks),
        in_specs=[slab(sl) for sl in range(N_SLABS)] + [
                  pl.BlockSpec((1, ATTN_TILE, 1), lambda bi, ti, c: (bi, ti, 0)),
                  pl.BlockSpec((1, LANES), lambda bi, ti, c: (0, 0)),
                  pl.BlockSpec((D_MODEL, D_MODEL), lambda bi, ti, c: (0, c))],
        out_specs=pl.BlockSpec((1, 1, HEADS, ATTN_TILE, HEAD_DIM), lambda bi, ti, c: (c, bi, 0, ti, 0)),
        out_shape=jax.ShapeDtypeStruct((n_chunks, b, HEADS, s, HEAD_DIM), BF16),
        scratch_shapes=[pltpu.VMEM((3, ATTN_TILE, D_MODEL), BF16),
                        pltpu.VMEM((3, ATTN_TILE, LANES), F32),
                        pltpu.VMEM((3, ATTN_TILE, LANES), F32)],
        compiler_params=pltpu.CompilerParams(
            dimension_semantics=("arbitrary", "arbitrary", "arbitrary"),
            vmem_limit_bytes=VMEM_LIMIT_BYTES),
        name="qkv_proj",
    )(*([x] * N_SLABS), pos3, freq, w_qkv)


def _attn_kernel(q0, k0, v0, q1, k1, v1, q2, k2, v2,
                 pk0, pv0, pk1, pv1, pk2, pv2, o_ref, acc_ref, m_ref, l_ref):
    has_prev = pl.program_id(1) > 0
    qi = lax.broadcasted_iota(jnp.int32, (ATTN_BLOCK, 2 * ATTN_BLOCK), 0)
    kj = lax.broadcasted_iota(jnp.int32, (ATTN_BLOCK, 2 * ATTN_BLOCK), 1)
    band = (kj >= qi) & (kj <= qi + ATTN_BLOCK)
    band_first = band & ((kj >= ATTN_BLOCK) | has_prev)
    contract_last = (((1,), (1,)), ((), ()))
    qs, ks, vs = (q0, q1, q2), (k0, k1, k2), (v0, v1, v2)
    pks, pvs = (pk0, pk1, pk2), (pv0, pv1, pv2)
    blk_rows = lambda r, i: r[0, 0, 0, i * ATTN_BLOCK:(i + 1) * ATTN_BLOCK, :]

    for gi, dilation in enumerate(ATTN_DILATIONS):
        n_streams = {1: 1, 4: 4, 16: 16}[dilation]
        for blk, (start, stride) in enumerate(_stream_blocks(dilation)):
            if blk < n_streams:
                k_prev, v_prev = blk_rows(pks[gi], blk), blk_rows(pvs[gi], blk)
            else:
                k_prev, v_prev = blk_rows(ks[gi], blk - n_streams), blk_rows(vs[gi], blk - n_streams)
            kcat = jnp.concatenate([k_prev, blk_rows(ks[gi], blk)], axis=0)
            vcat = jnp.concatenate([v_prev, blk_rows(vs[gi], blk)], axis=0)
            s = lax.dot_general(blk_rows(qs[gi], blk), kcat, contract_last,
                                preferred_element_type=F32)
            s = jnp.where(band_first if blk < n_streams else band, s, NEG_BIG)
            m = jnp.max(s, axis=-1, keepdims=True)
            p = jnp.exp(s - m)
            l = jnp.sum(p, axis=-1, keepdims=True)
            pv = jnp.dot(p.astype(BF16), vcat, preferred_element_type=F32)
            rows = pl.ds(start, ATTN_BLOCK, stride=stride) if stride > 1 else pl.ds(start, ATTN_BLOCK)
            acc_ref[gi, rows, :] = pv
            m_ref[gi, rows, :] = jnp.broadcast_to(m, (ATTN_BLOCK, LANES))
            l_ref[gi, rows, :] = jnp.broadcast_to(l, (ATTN_BLOCK, LANES))

    for c in range(ATTN_TILE // ATTN_BLOCK):
        rows = slice(c * ATTN_BLOCK, (c + 1) * ATTN_BLOCK)
        m0, m1, m2 = m_ref[0, rows, :], m_ref[1, rows, :], m_ref[2, rows, :]
        mx = jnp.maximum(jnp.maximum(m0, m1), m2)
        e0, e1, e2 = jnp.exp(m0 - mx), jnp.exp(m1 - mx), jnp.exp(m2 - mx)
        num = e0 * acc_ref[0, rows, :] + e1 * acc_ref[1, rows, :] + e2 * acc_ref[2, rows, :]
        den = e0 * l_ref[0, rows, :] + e1 * l_ref[1, rows, :] + e2 * l_ref[2, rows, :]
        o_ref[0, rows, :] = (num / den).astype(BF16)


def _attention(qkv):
    _, b, _, s, _ = qkv.shape
    n_tiles = s // ATTN_TILE

    def cur(chunk):
        return pl.BlockSpec((1, 1, 1, ATTN_TILE, HEAD_DIM), lambda bi, ti, hd: (chunk, bi, hd, ti, 0))

    def prev(chunk, rows):
        per_tile = ATTN_TILE // rows
        return pl.BlockSpec(
            (1, 1, 1, rows, HEAD_DIM),
            lambda bi, ti, hd: (chunk, bi, hd, jnp.maximum(ti * per_tile - 1, 0), 0))

    prev_rows = (ATTN_BLOCK, 4 * ATTN_BLOCK, ATTN_TILE)
    in_specs = [cur(c) for c in range(9)]
    operands = [qkv] * 9
    for gi in range(3):
        in_specs += [prev(3 * gi + 1, prev_rows[gi]), prev(3 * gi + 2, prev_rows[gi])]
        operands += [qkv, qkv]
    return pl.pallas_call(
        _attn_kernel,
        grid=(b, n_tiles, HEADS),
        in_specs=in_specs,
        out_specs=pl.BlockSpec((1, ATTN_TILE, HEAD_DIM), lambda bi, ti, hd: (bi, ti, hd)),
        out_shape=jax.ShapeDtypeStruct((b, s, HEADS * HEAD_DIM), BF16),
        scratch_shapes=[pltpu.VMEM((3, ATTN_TILE, LANES), F32)] * 3,
        compiler_params=pltpu.CompilerParams(
            dimension_semantics=("arbitrary", "arbitrary", "arbitrary"),
            vmem_limit_bytes=VMEM_LIMIT_BYTES),
        name="attention",
    )(*operands)


def kernel(x, positions, ev_w_in, ev_conv_w, ev_conv_b, ev_conv_ln_g, ev_conv_ln_b,
           ev_sgu_ln_g, ev_sgu_ln_b, ev_w_spatial, ev_b_spatial, ev_w_out,
           od_w_qkv, od_w_out, ffn_w_gate, ffn_w_up, ffn_w_down, ln_g, ln_b):
    b, s, d = x.shape
    assert (d, s % ATTN_TILE, (b * s) % TM_BACK) == (D_MODEL, 0, 0)
    pos3 = positions.reshape(b, s, 1)
    lane = jnp.arange(LANES)
    inv_freq = ROPE_THETA ** (-jnp.arange(0, ROT_DIM, 2, dtype=F32) / ROT_DIM)
    freq = jnp.where(lane < ROT_DIM, inv_freq[lane % (ROT_DIM // 2)], 0.0).reshape(1, LANES).astype(F32)

    for layer in range(DEPTH):
        i = layer // 2
        if layer % 2 == 0:
            cvec = jnp.stack([ev_conv_b[i], ev_conv_ln_g[i], ev_conv_ln_b[i]])
            svec = jnp.stack([ev_sgu_ln_g[i], ev_sgu_ln_b[i]])
            bias_full = jnp.repeat(ev_b_spatial[i].T, SGU_CH // SGU_GROUPS, axis=1)
            c = _even_front(x, ev_w_in[i].astype(BF16), ev_conv_w[i], cvec, svec,
                            ev_w_spatial[i], bias_full)
            w_out = ev_w_out[i]
        else:
            qkv = _qkv_proj(x, pos3, freq, od_w_qkv[i].astype(BF16))
            c = _attention(qkv)
            w_out = od_w_out[i]
        ln4 = jnp.stack([ln_g[layer, 0], ln_b[layer, 0], ln_g[layer, 1], ln_b[layer, 1]])
        x = _back(x.reshape(b * s, d), c.reshape(b * s, d), w_out.astype(BF16),
                  ffn_w_gate[layer].astype(BF16), ffn_w_up[layer].astype(BF16),
                  ffn_w_down[layer].astype(BF16), ln4).reshape(b, s, d)
    return x
```

```python
import functools

import jax
import jax.numpy as jnp
from jax import lax
from jax.experimental import pallas as pl
from jax.experimental.pallas import tpu as pltpu

F32 = jnp.float32
BF16 = jnp.bfloat16

D_MODEL = 1024
DEPTH = 4
CONV_CH = 512
CONV_WIDTH = 31
SGU_CH = 512
SGU_GROUPS = 8
SGU_CHUNK = 128
HEADS = 8
HEAD_DIM = 128
ATTN_BLOCK = 128
ATTN_DILATIONS = (1, 4, 16)
ROT_DIM = 32
ROPE_THETA = 500000.0
FFN_HIDDEN = 2816
DN_ALPHA = (2.0 * DEPTH) ** 0.25
LN_EPS = 1e-5
NEG_BIG = -1e30

LANES = 128
SUBLANES = 8
CONV_ROWS = 64
VMEM_LIMIT_BYTES = 56 * 1024 * 1024

TM_BACK = 512
TM_EVEN = 512
ATTN_TILE = 2048
N_SLABS = D_MODEL // LANES
QKV_ROWS = 512
CONV_HALO = 32


def _layer_norm(v, g, b):
    mu = jnp.mean(v, axis=-1, keepdims=True)
    c = v - mu
    var = jnp.mean(c * c, axis=-1, keepdims=True)
    return c * lax.rsqrt(var + LN_EPS) * g + b


def _resident(shape):
    nd = len(shape)
    return pl.BlockSpec(shape, lambda *_: (0,) * nd, pipeline_mode=pl.Buffered(1))


def _back_kernel(x_ref, c_ref, wo_ref, wg_ref, wu_ref, wd_ref, ln_ref, o_ref):
    x = x_ref[...]
    h = jnp.dot(c_ref[...], wo_ref[...], preferred_element_type=F32)
    x1 = _layer_norm(DN_ALPHA * x + h, ln_ref[0:1, :], ln_ref[1:2, :])
    x1b = x1.astype(BF16)
    g = jnp.dot(x1b, wg_ref[...], preferred_element_type=F32)
    u = jnp.dot(x1b, wu_ref[...], preferred_element_type=F32)
    a = (g * jax.nn.sigmoid(g) * u).astype(BF16)
    y = jnp.dot(a, wd_ref[...], preferred_element_type=F32)
    o_ref[...] = _layer_norm(DN_ALPHA * x1 + y, ln_ref[2:3, :], ln_ref[3:4, :])


def _back(x2d, c2d, w_out, w_gate, w_up, w_down, ln4):
    n = x2d.shape[0]
    tile = lambda: pl.BlockSpec((TM_BACK, D_MODEL), lambda i: (i, 0))
    return pl.pallas_call(
        _back_kernel,
        grid=(n // TM_BACK,),
        in_specs=[tile(), tile(),
                  _resident(w_out.shape), _resident(w_gate.shape),
                  _resident(w_up.shape), _resident(w_down.shape),
                  _resident(ln4.shape)],
        out_specs=tile(),
        out_shape=jax.ShapeDtypeStruct((n, D_MODEL), F32),
        compiler_params=pltpu.CompilerParams(
            dimension_semantics=("arbitrary",), vmem_limit_bytes=VMEM_LIMIT_BYTES),
        name="back",
    )(x2d, c2d, w_out, w_gate, w_up, w_down, ln4)


def _even_kernel(x_ref, win_ref, cw_ref, cvec_ref, svec_ref, wsp_ref, bsp_ref, o_ref, abuf_ref):
    tm = TM_EVEN
    h = jnp.dot(x_ref[0].astype(BF16), win_ref[...], preferred_element_type=F32)

    a = h[:, :CONV_CH] * jax.nn.sigmoid(h[:, CONV_CH:2 * CONV_CH])

    @pl.when(pl.program_id(1) == 0)
    def _():
        abuf_ref[0, 0:CONV_HALO, :] = jnp.zeros((CONV_HALO, CONV_CH), F32)

    abuf_ref[0, CONV_HALO:CONV_HALO + tm, :] = a
    shifted_rows = tm + CONV_HALO - SUBLANES
    for r in range(1, SUBLANES):
        abuf_ref[r, 0:shifted_rows, :] = abuf_ref[0, r:r + shifted_rows, :]
    base = CONV_HALO - (CONV_WIDTH - 1)

    for r0 in range(0, tm, CONV_ROWS):
        conv = jnp.broadcast_to(cvec_ref[0:1, :], (CONV_ROWS, CONV_CH))
        for j in range(CONV_WIDTH):
            q, r = divmod(base + j, SUBLANES)
            conv = conv + cw_ref[j:j + 1, :] * abuf_ref[r, r0 + SUBLANES * q:r0 + SUBLANES * q + CONV_ROWS, :]
        an = _layer_norm(conv, cvec_ref[1:2, :], cvec_ref[2:3, :])
        o_ref[0, r0:r0 + CONV_ROWS, 0:CONV_CH] = (an * jax.nn.sigmoid(an)).astype(BF16)
    abuf_ref[0, 0:CONV_HALO, :] = abuf_ref[0, tm:tm + CONV_HALO, :]

    z = h[:, 2 * CONV_CH:]
    z = 0.5 * z * (1.0 + lax.erf(z * (2.0 ** -0.5)))
    u = z[:, :SGU_CH]
    v = _layer_norm(z[:, SGU_CH:], svec_ref[0:1, :], svec_ref[1:2, :])
    row = lax.broadcasted_iota(jnp.int32, (SGU_CHUNK, SGU_CHUNK), 0)
    col = lax.broadcasted_iota(jnp.int32, (SGU_CHUNK, SGU_CHUNK), 1)
    tril = col <= row
    lane = lax.broadcasted_iota(jnp.int32, (tm, LANES), 1)
    low_half = lane < (LANES // 2)
    for p in range(SGU_GROUPS // 2):
        w_pair = jnp.concatenate(
            [jnp.where(tril, wsp_ref[2 * p], 0.0), jnp.where(tril, wsp_ref[2 * p + 1], 0.0)],
            axis=1).astype(BF16)
        v_slab = v[:, p * LANES:(p + 1) * LANES]
        v_lo = jnp.where(low_half, v_slab, 0.0).astype(BF16)
        v_hi = jnp.where(low_half, 0.0, v_slab).astype(BF16)
        for c in range(tm // SGU_CHUNK):
            rows = slice(c * SGU_CHUNK, (c + 1) * SGU_CHUNK)
            rhs = jnp.concatenate([v_lo[rows], v_hi[rows]], axis=0)
            sv = jnp.dot(w_pair, rhs, preferred_element_type=F32) + bsp_ref[:, p * LANES:(p + 1) * LANES]
            o_ref[0, rows, CONV_CH + p * LANES:CONV_CH + (p + 1) * LANES] = (
                u[rows, p * LANES:(p + 1) * LANES] * sv).astype(BF16)


def _even_front(x, w_in, conv_w, cvec, svec, w_spatial, bias_full):
    b, s, _ = x.shape
    return pl.pallas_call(
        _even_kernel,
        grid=(b, s // TM_EVEN),
        in_specs=[pl.BlockSpec((1, TM_EVEN, D_MODEL), lambda bi, i: (bi, i, 0)),
                  _resident(w_in.shape), _resident(conv_w.shape), _resident(cvec.shape),
                  _resident(svec.shape), _resident(w_spatial.shape), _resident(bias_full.shape)],
        out_specs=pl.BlockSpec((1, TM_EVEN, D_MODEL), lambda bi, i: (bi, i, 0)),
        out_shape=jax.ShapeDtypeStruct((b, s, D_MODEL), BF16),
        scratch_shapes=[pltpu.VMEM((SUBLANES, CONV_HALO + TM_EVEN, CONV_CH), F32)],
        compiler_params=pltpu.CompilerParams(
            dimension_semantics=("arbitrary", "arbitrary"), vmem_limit_bytes=VMEM_LIMIT_BYTES),
        name="even_front",
    )(x, w_in, conv_w, cvec, svec, w_spatial, bias_full)


def _stream_blocks(dilation):
    if dilation == 1:
        return [(n * ATTN_BLOCK, 1) for n in range(ATTN_TILE // ATTN_BLOCK)]
    if dilation == 4:
        return [(n * ATTN_BLOCK * 4 + rho, 4) for n in range(4) for rho in range(4)]
    return [(rho, 16) for rho in range(16)]


def _qkv_kernel(*refs):
    x_slabs = refs[:N_SLABS]
    pos_ref, freq_ref, w_ref, o_ref, xp_ref, cos_ref, sin_ref = refs[N_SLABS:]
    c = pl.program_id(2)

    @pl.when(c == 0)
    def _():
        ang = pos_ref[0].astype(F32) * freq_ref[...]
        cos_ref[0] = jnp.cos(ang)
        sin_ref[0] = jnp.sin(ang)
        for gi, dilation in enumerate(ATTN_DILATIONS):
            for blk, (start, stride) in enumerate(_stream_blocks(dilation)):
                rows = pl.ds(start, ATTN_BLOCK, stride=stride) if stride > 1 else pl.ds(start, ATTN_BLOCK)
                dst = slice(blk * ATTN_BLOCK, (blk + 1) * ATTN_BLOCK)
                for sl in range(N_SLABS):
                    xp_ref[gi, dst, sl * LANES:(sl + 1) * LANES] = x_slabs[sl][0, rows, :].astype(BF16)
                if gi > 0:
                    cos_ref[gi, dst, :] = cos_ref[0, rows, :]
                    sin_ref[gi, dst, :] = sin_ref[0, rows, :]

    gi = c // 3
    t = c % 3
    scale = jnp.where(t == 0, HEAD_DIM ** -0.5, 1.0).astype(F32)
    rotate = t < 2
    lane = lax.broadcasted_iota(jnp.int32, (QKV_ROWS, LANES), 1)
    half = ROT_DIM // 2
    for rc in range(ATTN_TILE // QKV_ROWS):
        rows = slice(rc * QKV_ROWS, (rc + 1) * QKV_ROWS)
        y = jnp.dot(xp_ref[gi, rows, :], w_ref[...], preferred_element_type=F32)
        cos = jnp.where(rotate, cos_ref[gi, rows, :], 1.0) * scale
        sin = jnp.where(rotate, sin_ref[gi, rows, :], 0.0) * scale
        sin_up = jnp.where((lane >= half) & (lane < ROT_DIM), sin, 0.0)
        sin_dn = jnp.where(lane < half, -sin, 0.0)
        for hd in range(HEADS):
            yh = y[:, hd * HEAD_DIM:(hd + 1) * HEAD_DIM]
            rot = (yh * cos + pltpu.roll(yh, half, axis=1) * sin_up
                   + pltpu.roll(yh, LANES - half, axis=1) * sin_dn)
            o_ref[0, 0, hd, rows, :] = rot.astype(BF16)


def _qkv_proj(x, pos3, freq, w_qkv):
    b, s, _ = x.shape
    n_chunks = w_qkv.shape[1] // D_MODEL
    slab = lambda sl: pl.BlockSpec((1, ATTN_TILE, LANES), lambda bi, ti, c: (bi, ti, sl))
    return pl.pallas_call(
        _qkv_kernel,
        grid=(b, s // ATTN_TILE, n_chunks),
        in_specs=[slab(sl) for sl in range(N_SLABS)] + [
                  pl.BlockSpec((1, ATTN_TILE, 1), lambda bi, ti, c: (bi, ti, 0)),
                  pl.BlockSpec((1, LANES), lambda bi, ti, c: (0, 0)),
                  pl.BlockSpec((D_MODEL, D_MODEL), lambda bi, ti, c: (0, c))],
        out_specs=pl.BlockSpec((1, 1, HEADS, ATTN_TILE, HEAD_DIM), lambda bi, ti, c: (c, bi, 0, ti, 0)),
        out_shape=jax.ShapeDtypeStruct((n_chunks, b, HEADS, s, HEAD_DIM), BF16),
        scratch_shapes=[pltpu.VMEM((3, ATTN_TILE, D_MODEL), BF16),
                        pltpu.VMEM((3, ATTN_TILE, LANES), F32),
                        pltpu.VMEM((3, ATTN_TILE, LANES), F32)],
        compiler_params=pltpu.CompilerParams(
            dimension_semantics=("arbitrary", "arbitrary", "arbitrary"),
            vmem_limit_bytes=VMEM_LIMIT_BYTES),
        name="qkv_proj",
    )(*([x] * N_SLABS), pos3, freq, w_qkv)


def _attn_kernel(q0, k0, v0, q1, k1, v1, q2, k2, v2,
                 pk0, pv0, pk1, pv1, pk2, pv2, o_ref, acc_ref, m_ref, l_ref):
    has_prev = pl.program_id(1) > 0
    qi = lax.broadcasted_iota(jnp.int32, (ATTN_BLOCK, 2 * ATTN_BLOCK), 0)
    kj = lax.broadcasted_iota(jnp.int32, (ATTN_BLOCK, 2 * ATTN_BLOCK), 1)
    band = (kj >= qi) & (kj <= qi + ATTN_BLOCK)
    band_first = band & ((kj >= ATTN_BLOCK) | has_prev)
    contract_last = (((1,), (1,)), ((), ()))
    ones = jnp.ones((2 * ATTN_BLOCK, LANES), BF16)
    qs, ks, vs = (q0, q1, q2), (k0, k1, k2), (v0, v1, v2)
    pks, pvs = (pk0, pk1, pk2), (pv0, pv1, pv2)
    blk_rows = lambda r, i: r[0, 0, 0, i * ATTN_BLOCK:(i + 1) * ATTN_BLOCK, :]

    for gi, dilation in enumerate(ATTN_DILATIONS):
        n_streams = {1: 1, 4: 4, 16: 16}[dilation]
        for blk, (start, stride) in enumerate(_stream_blocks(dilation)):
            if blk < n_streams:
                k_prev, v_prev = blk_rows(pks[gi], blk), blk_rows(pvs[gi], blk)
            else:
                k_prev, v_prev = blk_rows(ks[gi], blk - n_streams), blk_rows(vs[gi], blk - n_streams)
            kcat = jnp.concatenate([k_prev, blk_rows(ks[gi], blk)], axis=0)
            vext = jnp.concatenate(
                [jnp.concatenate([v_prev, blk_rows(vs[gi], blk)], axis=0), ones], axis=1)
            s = lax.dot_general(blk_rows(qs[gi], blk), kcat, contract_last,
                                preferred_element_type=F32)
            s = jnp.where(band_first if blk < n_streams else band, s, NEG_BIG)
            m = jnp.max(s, axis=-1, keepdims=True)
            p = jnp.exp(s - m)
            pvl = jnp.dot(p.astype(BF16), vext, preferred_element_type=F32)
            rows = pl.ds(start, ATTN_BLOCK, stride=stride) if stride > 1 else pl.ds(start, ATTN_BLOCK)
            acc_ref[gi, rows, :] = pvl[:, :HEAD_DIM]
            l_ref[gi, rows, :] = pvl[:, HEAD_DIM:]
            m_ref[gi, rows, :] = jnp.broadcast_to(m, (ATTN_BLOCK, LANES))

    for c in range(ATTN_TILE // ATTN_BLOCK):
        rows = slice(c * ATTN_BLOCK, (c + 1) * ATTN_BLOCK)
        m0, m1, m2 = m_ref[0, rows, :], m_ref[1, rows, :], m_ref[2, rows, :]
        mx = jnp.maximum(jnp.maximum(m0, m1), m2)
        e0, e1, e2 = jnp.exp(m0 - mx), jnp.exp(m1 - mx), jnp.exp(m2 - mx)
        num = e0 * acc_ref[0, rows, :] + e1 * acc_ref[1, rows, :] + e2 * acc_ref[2, rows, :]
        den = e0 * l_ref[0, rows, :] + e1 * l_ref[1, rows, :] + e2 * l_ref[2, rows, :]
        o_ref[0, rows, :] = (num / den).astype(BF16)


def _attention(qkv):
    _, b, _, s, _ = qkv.shape
    n_tiles = s // ATTN_TILE

    def cur(chunk):
        return pl.BlockSpec((1, 1, 1, ATTN_TILE, HEAD_DIM), lambda bi, ti, hd: (chunk, bi, hd, ti, 0))

    def prev(chunk, rows):
        per_tile = ATTN_TILE // rows
        return pl.BlockSpec(
            (1, 1, 1, rows, HEAD_DIM),
            lambda bi, ti, hd: (chunk, bi, hd, jnp.maximum(ti * per_tile - 1, 0), 0))

    prev_rows = (ATTN_BLOCK, 4 * ATTN_BLOCK, ATTN_TILE)
    in_specs = [cur(c) for c in range(9)]
    operands = [qkv] * 9
    for gi in range(3):
        in_specs += [prev(3 * gi + 1, prev_rows[gi]), prev(3 * gi + 2, prev_rows[gi])]
        operands += [qkv, qkv]
    return pl.pallas_call(
        _attn_kernel,
        grid=(b, n_tiles, HEADS),
        in_specs=in_specs,
        out_specs=pl.BlockSpec((1, ATTN_TILE, HEAD_DIM), lambda bi, ti, hd: (bi, ti, hd)),
        out_shape=jax.ShapeDtypeStruct((b, s, HEADS * HEAD_DIM), BF16),
        scratch_shapes=[pltpu.VMEM((3, ATTN_TILE, LANES), F32)] * 3,
        compiler_params=pltpu.CompilerParams(
            dimension_semantics=("arbitrary", "arbitrary", "arbitrary"),
            vmem_limit_bytes=VMEM_LIMIT_BYTES),
        name="attention",
    )(*operands)


def kernel(x, positions, ev_w_in, ev_conv_w, ev_conv_b, ev_conv_ln_g, ev_conv_ln_b,
           ev_sgu_ln_g, ev_sgu_ln_b, ev_w_spatial, ev_b_spatial, ev_w_out,
           od_w_qkv, od_w_out, ffn_w_gate, ffn_w_up, ffn_w_down, ln_g, ln_b):
    b, s, d = x.shape
    assert (d, s % ATTN_TILE, (b * s) % TM_BACK) == (D_MODEL, 0, 0)
    pos3 = positions.reshape(b, s, 1)
    lane = jnp.arange(LANES)
    inv_freq = ROPE_THETA ** (-jnp.arange(0, ROT_DIM, 2, dtype=F32) / ROT_DIM)
    freq = jnp.where(lane < ROT_DIM, inv_freq[lane % (ROT_DIM // 2)], 0.0).reshape(1, LANES).astype(F32)

    for layer in range(DEPTH):
        i = layer // 2
        if layer % 2 == 0:
            cvec = jnp.stack([ev_conv_b[i], ev_conv_ln_g[i], ev_conv_ln_b[i]])
            svec = jnp.stack([ev_sgu_ln_g[i], ev_sgu_ln_b[i]])
            bias_full = jnp.repeat(ev_b_spatial[i].T, SGU_CH // SGU_GROUPS, axis=1)
            c = _even_front(x, ev_w_in[i].astype(BF16), ev_conv_w[i], cvec, svec,
                            ev_w_spatial[i], bias_full)
            w_out = ev_w_out[i]
        else:
            qkv = _qkv_proj(x, pos3, freq, od_w_qkv[i].astype(BF16))
            c = _attention(qkv)
            w_out = od_w_out[i]
        ln4 = jnp.stack([ln_g[layer, 0], ln_b[layer, 0], ln_g[layer, 1], ln_b[layer, 1]])
        x = _back(x.reshape(b * s, d), c.reshape(b * s, d), w_out.astype(BF16),
                  ffn_w_gate[layer].astype(BF16), ffn_w_up[layer].astype(BF16),
                  ffn_w_down[layer].astype(BF16), ln4).reshape(b, s, d)
    return x
```

```python
import functools

import jax
import jax.numpy as jnp
import numpy as np
from jax import lax
from jax.experimental import pallas as pl
from jax.experimental.pallas import tpu as pltpu

F32 = jnp.float32
BF16 = jnp.bfloat16

D_MODEL = 1024
DEPTH = 4
CONV_CH = 512
CONV_WIDTH = 31
SGU_CH = 512
SGU_GROUPS = 8
SGU_CHUNK = 128
HEADS = 8
HEAD_DIM = 128
ATTN_BLOCK = 128
ATTN_DILATIONS = (1, 4, 16)
ROT_DIM = 32
ROPE_THETA = 500000.0
FFN_HIDDEN = 2816
DN_ALPHA = (2.0 * DEPTH) ** 0.25
LN_EPS = 1e-5
NEG_BIG = -1e30

LANES = 128
SUBLANES = 8
CONV_ROWS = 64
VMEM_LIMIT_BYTES = 56 * 1024 * 1024

TM_BACK = 1024
BACK_ROWS = 256
TM_EVEN = 512
ATTN_TILE = 2048
N_SLABS = D_MODEL // LANES
R2_LANE = LANES // 2
QKV_ROWS = 512
PERM_BLOCKS_PER_STEP = 6
CONV_HALO = 32


def _layer_norm(v, g, b):
    mu = jnp.mean(v, axis=-1, keepdims=True)
    c = v - mu
    var = jnp.mean(c * c, axis=-1, keepdims=True)
    return c * lax.rsqrt(var + LN_EPS) * g + b


def _resident(shape):
    nd = len(shape)
    return pl.BlockSpec(shape, lambda *_: (0,) * nd, pipeline_mode=pl.Buffered(1))


def _back_kernel(x_ref, c_ref, wo_ref, wg_ref, wu_ref, wd_ref, ln_ref, o_ref):
    rows = [slice(i * BACK_ROWS, (i + 1) * BACK_ROWS) for i in range(TM_BACK // BACK_ROWS)]
    hs = [jnp.dot(c_ref[r, :], wo_ref[...], preferred_element_type=F32) for r in rows]
    x1s = [_layer_norm(DN_ALPHA * x_ref[r, :] + h, ln_ref[0:1, :], ln_ref[1:2, :]) for r, h in zip(rows, hs)]
    acts = []
    for x1 in x1s:
        x1b = x1.astype(BF16)
        g = jnp.dot(x1b, wg_ref[...], preferred_element_type=F32)
        u = jnp.dot(x1b, wu_ref[...], preferred_element_type=F32)
        acts.append((g * jax.nn.sigmoid(g) * u).astype(BF16))
    for r, x1, a in zip(rows, x1s, acts):
        y = jnp.dot(a, wd_ref[...], preferred_element_type=F32)
        o_ref[r, :] = _layer_norm(DN_ALPHA * x1 + y, ln_ref[2:3, :], ln_ref[3:4, :])


def _back(x2d, c2d, w_out, w_gate, w_up, w_down, ln4):
    n = x2d.shape[0]
    tile = lambda: pl.BlockSpec((TM_BACK, D_MODEL), lambda i: (i, 0))
    return pl.pallas_call(
        _back_kernel,
        grid=(n // TM_BACK,),
        in_specs=[tile(), tile(),
                  _resident(w_out.shape), _resident(w_gate.shape),
                  _resident(w_up.shape), _resident(w_down.shape),
                  _resident(ln4.shape)],
        out_specs=tile(),
        out_shape=jax.ShapeDtypeStruct((n, D_MODEL), F32),
        compiler_params=pltpu.CompilerParams(
            dimension_semantics=("arbitrary",), vmem_limit_bytes=VMEM_LIMIT_BYTES),
        name="back",
    )(x2d, c2d, w_out, w_gate, w_up, w_down, ln4)


def _even_kernel(x_ref, win_ref, cw_ref, cvec_ref, svec_ref, wsp_ref, bsp_ref, o_ref, abuf_ref):
    tm = TM_EVEN
    h = jnp.dot(x_ref[0].astype(BF16), win_ref[...], preferred_element_type=F32)

    a = h[:, :CONV_CH] * jax.nn.sigmoid(h[:, CONV_CH:2 * CONV_CH])

    @pl.when(pl.program_id(1) == 0)
    def _():
        abuf_ref[0, 0:CONV_HALO, :] = jnp.zeros((CONV_HALO, CONV_CH), F32)

    abuf_ref[0, CONV_HALO:CONV_HALO + tm, :] = a
    shifted_rows = tm + CONV_HALO - SUBLANES
    for r in range(1, SUBLANES):
        abuf_ref[r, 0:shifted_rows, :] = abuf_ref[0, r:r + shifted_rows, :]
    base = CONV_HALO - (CONV_WIDTH - 1)

    for r0 in range(0, tm, CONV_ROWS):
        conv = jnp.broadcast_to(cvec_ref[0:1, :], (CONV_ROWS, CONV_CH))
        for j in range(CONV_WIDTH):
            q, r = divmod(base + j, SUBLANES)
            conv = conv + cw_ref[j:j + 1, :] * abuf_ref[r, r0 + SUBLANES * q:r0 + SUBLANES * q + CONV_ROWS, :]
        an = _layer_norm(conv, cvec_ref[1:2, :], cvec_ref[2:3, :])
        o_ref[0, r0:r0 + CONV_ROWS, 0:CONV_CH] = (an * jax.nn.sigmoid(an)).astype(BF16)
    abuf_ref[0, 0:CONV_HALO, :] = abuf_ref[0, tm:tm + CONV_HALO, :]

    z = h[:, 2 * CONV_CH:]
    z = 0.5 * z * (1.0 + lax.erf(z * (2.0 ** -0.5)))
    u = z[:, :SGU_CH]
    v = _layer_norm(z[:, SGU_CH:], svec_ref[0:1, :], svec_ref[1:2, :])
    row = lax.broadcasted_iota(jnp.int32, (SGU_CHUNK, SGU_CHUNK), 0)
    col = lax.broadcasted_iota(jnp.int32, (SGU_CHUNK, SGU_CHUNK), 1)
    tril = col <= row
    lane = lax.broadcasted_iota(jnp.int32, (tm, LANES), 1)
    low_half = lane < (LANES // 2)
    for p in range(SGU_GROUPS // 2):
        w_pair = jnp.concatenate(
            [jnp.where(tril, wsp_ref[2 * p], 0.0), jnp.where(tril, wsp_ref[2 * p + 1], 0.0)],
            axis=1).astype(BF16)
        v_slab = v[:, p * LANES:(p + 1) * LANES]
        v_lo = jnp.where(low_half, v_slab, 0.0).astype(BF16)
        v_hi = jnp.where(low_half, 0.0, v_slab).astype(BF16)
        for c in range(tm // SGU_CHUNK):
            rows = slice(c * SGU_CHUNK, (c + 1) * SGU_CHUNK)
            rhs = jnp.concatenate([v_lo[rows], v_hi[rows]], axis=0)
            sv = jnp.dot(w_pair, rhs, preferred_element_type=F32) + bsp_ref[:, p * LANES:(p + 1) * LANES]
            o_ref[0, rows, CONV_CH + p * LANES:CONV_CH + (p + 1) * LANES] = (
                u[rows, p * LANES:(p + 1) * LANES] * sv).astype(BF16)


def _even_front(x, w_in, conv_w, cvec, svec, w_spatial, bias_full):
    b, s, _ = x.shape
    return pl.pallas_call(
        _even_kernel,
        grid=(b, s // TM_EVEN),
        in_specs=[pl.BlockSpec((1, TM_EVEN, D_MODEL), lambda bi, i: (bi, i, 0)),
                  _resident(w_in.shape), _resident(conv_w.shape), _resident(cvec.shape),
                  _resident(svec.shape), _resident(w_spatial.shape), _resident(bias_full.shape)],
        out_specs=pl.BlockSpec((1, TM_EVEN, D_MODEL), lambda bi, i: (bi, i, 0)),
        out_shape=jax.ShapeDtypeStruct((b, s, D_MODEL), BF16),
        scratch_shapes=[pltpu.VMEM((SUBLANES, CONV_HALO + TM_EVEN, CONV_CH), F32)],
        compiler_params=pltpu.CompilerParams(
            dimension_semantics=("arbitrary", "arbitrary"), vmem_limit_bytes=VMEM_LIMIT_BYTES),
        name="even_front",
    )(x, w_in, conv_w, cvec, svec, w_spatial, bias_full)


def _stream_blocks(dilation):
    if dilation == 1:
        return [(n * ATTN_BLOCK, 1) for n in range(ATTN_TILE // ATTN_BLOCK)]
    if dilation == 4:
        return [(n * ATTN_BLOCK * 4 + rho, 4) for n in range(4) for rho in range(4)]
    return [(rho, 16) for rho in range(16)]


def _qkv_kernel(*refs):
    x_slabs = refs[:N_SLABS]
    posc_ref, freq_ref, w_ref, o_ref = refs[N_SLABS:N_SLABS + 4]
    xp_refs = refs[N_SLABS + 4:N_SLABS + 7]
    cos_ref, sin_ref = refs[N_SLABS + 7:]
    c = pl.program_id(2)
    half = ROT_DIM // 2

    @pl.when(c == 0)
    def _():
        ang = posc_ref[0].astype(F32) * freq_ref[...]
        cos_c, sin_c = jnp.cos(ang), jnp.sin(ang)
        lane_c = lax.broadcasted_iota(jnp.int32, ang.shape, 1)

        def spread(tab, s):
            roll = lambda shift: tab if shift % LANES == 0 else pltpu.roll(tab, shift % LANES, axis=1)
            return roll(LANES - half * s), roll(LANES + R2_LANE - half * s)

        in_r1 = lane_c < half
        in_r2 = (lane_c >= R2_LANE) & (lane_c < R2_LANE + half)
        for s in range(SUBLANES):
            c1, c2 = spread(cos_c, s)
            s1, s2 = spread(sin_c, s)
            rows = pl.ds(s, ATTN_TILE // SUBLANES, stride=SUBLANES)
            cos_ref[0, rows, :] = jnp.where(in_r1, c1, jnp.where(in_r2, c2, 1.0))
            sin_ref[0, rows, :] = jnp.where(in_r1, -s1, jnp.where(in_r2, s2, 0.0))
        for sl in range(N_SLABS):
            xp_refs[0][:, sl * LANES:(sl + 1) * LANES] = x_slabs[sl][0].astype(BF16)
        for gi in (1, 2):
            for blk, (start, stride) in enumerate(_stream_blocks(ATTN_DILATIONS[gi])):
                rows = pl.ds(start, ATTN_BLOCK, stride=stride)
                dst = slice(blk * ATTN_BLOCK, (blk + 1) * ATTN_BLOCK)
                cos_ref[gi, dst, :] = cos_ref[0, rows, :]
                sin_ref[gi, dst, :] = sin_ref[0, rows, :]

    t = c % 3
    scale = jnp.where(t == 0, HEAD_DIM ** -0.5, 1.0).astype(F32)
    rotate = t < 2

    def project(gi):
        def permute_next(i):
            stride = ATTN_DILATIONS[gi + 1]
            blk = jnp.minimum(t * PERM_BLOCKS_PER_STEP + i, ATTN_TILE // ATTN_BLOCK - 1)
            if stride == 4:
                start = (blk >> 2) * (4 * ATTN_BLOCK) + (blk & 3)
            else:
                start = blk
            dst = pl.ds(pl.multiple_of(blk * ATTN_BLOCK, ATTN_BLOCK), ATTN_BLOCK)
            for sl in range(N_SLABS):
                xp_refs[gi + 1][dst, sl * LANES:(sl + 1) * LANES] = (
                    x_slabs[sl][0, pl.ds(start, ATTN_BLOCK, stride=stride), :].astype(BF16))

        n_rc = ATTN_TILE // QKV_ROWS
        for rc in range(n_rc):
            rows = slice(rc * QKV_ROWS, (rc + 1) * QKV_ROWS)
            if gi + 1 < len(ATTN_DILATIONS):
                for i in range(rc, PERM_BLOCKS_PER_STEP, n_rc):
                    permute_next(i)
            y = jnp.dot(xp_refs[gi][rows, :], w_ref[...], preferred_element_type=F32)
            cos = jnp.where(rotate, cos_ref[gi, rows, :], 1.0) * scale
            sin = jnp.where(rotate, sin_ref[gi, rows, :], 0.0) * scale
            for hd in range(HEADS):
                yh = y[:, hd * HEAD_DIM:(hd + 1) * HEAD_DIM]
                o_ref[0, 0, hd, rows, :] = (yh * cos + pltpu.roll(yh, R2_LANE, axis=1) * sin).astype(BF16)

    for gi in range(len(ATTN_DILATIONS)):
        pl.when(c // 3 == gi)(functools.partial(project, gi))


def _qkv_proj(x, posc, freq, w_qkv):
    b, s, _ = x.shape
    n_chunks = w_qkv.shape[1] // D_MODEL
    slab = lambda sl: pl.BlockSpec((1, ATTN_TILE, LANES), lambda bi, ti, c: (bi, ti, sl))
    return pl.pallas_call(
        _qkv_kernel,
        grid=(b, s // ATTN_TILE, n_chunks),
        in_specs=[slab(sl) for sl in range(N_SLABS)] + [
                  pl.BlockSpec((1, ATTN_TILE // SUBLANES, LANES), lambda bi, ti, c: (bi, ti, 0)),
                  pl.BlockSpec((1, LANES), lambda bi, ti, c: (0, 0)),
                  pl.BlockSpec((D_MODEL, D_MODEL), lambda bi, ti, c: (0, c))],
        out_specs=pl.BlockSpec((1, 1, HEADS, ATTN_TILE, HEAD_DIM), lambda bi, ti, c: (c, bi, 0, ti, 0)),
        out_shape=jax.ShapeDtypeStruct((n_chunks, b, HEADS, s, HEAD_DIM), BF16),
        scratch_shapes=[pltpu.VMEM((ATTN_TILE, D_MODEL), BF16)] * 3 + [
                        pltpu.VMEM((3, ATTN_TILE, LANES), F32),
                        pltpu.VMEM((3, ATTN_TILE, LANES), F32)],
        compiler_params=pltpu.CompilerParams(
            dimension_semantics=("arbitrary", "arbitrary", "arbitrary"),
            vmem_limit_bytes=VMEM_LIMIT_BYTES),
        name="qkv_proj",
    )(*([x] * N_SLABS), posc, freq, w_qkv)


def _attn_kernel(q0, k0, v0, q1, k1, v1, q2, k2, v2,
                 pk0, pv0, pk1, pv1, pk2, pv2, o_ref, acc_ref, m_ref, l_ref):
    has_prev = pl.program_id(1) > 0
    qi = lax.broadcasted_iota(jnp.int32, (ATTN_BLOCK, 2 * ATTN_BLOCK), 0)
    kj = lax.broadcasted_iota(jnp.int32, (ATTN_BLOCK, 2 * ATTN_BLOCK), 1)
    band = (kj >= qi) & (kj <= qi + ATTN_BLOCK)
    band_first = band & ((kj >= ATTN_BLOCK) | has_prev)
    contract_last = (((1,), (1,)), ((), ()))
    ones = jnp.ones((2 * ATTN_BLOCK, LANES), BF16)
    qs, ks, vs = (q0, q1, q2), (k0, k1, k2), (v0, v1, v2)
    pks, pvs = (pk0, pk1, pk2), (pv0, pv1, pv2)
    blk_rows = lambda r, i: r[0, 0, 0, i * ATTN_BLOCK:(i + 1) * ATTN_BLOCK, :]

    for gi, dilation in enumerate(ATTN_DILATIONS):
        n_streams = {1: 1, 4: 4, 16: 16}[dilation]
        for blk, (start, stride) in enumerate(_stream_blocks(dilation)):
            if blk < n_streams:
                k_prev, v_prev = blk_rows(pks[gi], blk), blk_rows(pvs[gi], blk)
            else:
                k_prev, v_prev = blk_rows(ks[gi], blk - n_streams), blk_rows(vs[gi], blk - n_streams)
            kcat = jnp.concatenate([k_prev, blk_rows(ks[gi], blk)], axis=0)
            vext = jnp.concatenate(
                [jnp.concatenate([v_prev, blk_rows(vs[gi], blk)], axis=0), ones], axis=1)
            s = lax.dot_general(blk_rows(qs[gi], blk), kcat, contract_last,
                                preferred_element_type=F32)
            s = jnp.where(band_first if blk < n_streams else band, s, NEG_BIG)
            m = jnp.max(s, axis=-1, keepdims=True)
            p = jnp.exp(s - m)
            pvl = jnp.dot(p.astype(BF16), vext, preferred_element_type=F32)
            rows = pl.ds(start, ATTN_BLOCK, stride=stride) if stride > 1 else pl.ds(start, ATTN_BLOCK)
            acc_ref[gi, rows, :] = pvl[:, :HEAD_DIM]
            l_ref[gi, rows, :] = pvl[:, HEAD_DIM:]
            m_ref[gi, rows, :] = jnp.broadcast_to(m, (ATTN_BLOCK, LANES))

    for c in range(ATTN_TILE // ATTN_BLOCK):
        rows = slice(c * ATTN_BLOCK, (c + 1) * ATTN_BLOCK)
        m0, m1, m2 = m_ref[0, rows, :], m_ref[1, rows, :], m_ref[2, rows, :]
        mx = jnp.maximum(jnp.maximum(m0, m1), m2)
        e0, e1, e2 = jnp.exp(m0 - mx), jnp.exp(m1 - mx), jnp.exp(m2 - mx)
        num = e0 * acc_ref[0, rows, :] + e1 * acc_ref[1, rows, :] + e2 * acc_ref[2, rows, :]
        den = e0 * l_ref[0, rows, :] + e1 * l_ref[1, rows, :] + e2 * l_ref[2, rows, :]
        o_ref[0, rows, :] = (num / den).astype(BF16)


def _attention(qkv):
    _, b, _, s, _ = qkv.shape
    n_tiles = s // ATTN_TILE

    def cur(chunk):
        return pl.BlockSpec((1, 1, 1, ATTN_TILE, HEAD_DIM), lambda bi, ti, hd: (chunk, bi, hd, ti, 0))

    def prev(chunk, rows):
        per_tile = ATTN_TILE // rows
        return pl.BlockSpec(
            (1, 1, 1, rows, HEAD_DIM),
            lambda bi, ti, hd: (chunk, bi, hd, jnp.maximum(ti * per_tile - 1, 0), 0))

    prev_rows = (ATTN_BLOCK, 4 * ATTN_BLOCK, ATTN_TILE)
    in_specs = [cur(c) for c in range(9)]
    operands = [qkv] * 9
    for gi in range(3):
        in_specs += [prev(3 * gi + 1, prev_rows[gi]), prev(3 * gi + 2, prev_rows[gi])]
        operands += [qkv, qkv]
    return pl.pallas_call(
        _attn_kernel,
        grid=(b, n_tiles, HEADS),
        in_specs=in_specs,
        out_specs=pl.BlockSpec((1, ATTN_TILE, HEAD_DIM), lambda bi, ti, hd: (bi, ti, hd)),
        out_shape=jax.ShapeDtypeStruct((b, s, HEADS * HEAD_DIM), BF16),
        scratch_shapes=[pltpu.VMEM((3, ATTN_TILE, LANES), F32)] * 3,
        compiler_params=pltpu.CompilerParams(
            dimension_semantics=("arbitrary", "arbitrary", "arbitrary"),
            vmem_limit_bytes=VMEM_LIMIT_BYTES),
        name="attention",
    )(*operands)


def _qk_column_order(n_cols):
    half = ROT_DIM // 2
    head = np.concatenate([np.arange(0, half), np.arange(ROT_DIM, R2_LANE + half),
                           np.arange(half, ROT_DIM), np.arange(R2_LANE + half, HEAD_DIM)])
    cols = np.arange(n_cols).reshape(-1, 3, HEADS, HEAD_DIM)
    cols[:, :2] = cols[:, :2][..., head]
    return cols.reshape(-1)


def kernel(x, positions, ev_w_in, ev_conv_w, ev_conv_b, ev_conv_ln_g, ev_conv_ln_b,
           ev_sgu_ln_g, ev_sgu_ln_b, ev_w_spatial, ev_b_spatial, ev_w_out,
           od_w_qkv, od_w_out, ffn_w_gate, ffn_w_up, ffn_w_down, ln_g, ln_b):
    b, s, d = x.shape
    assert (d, s % ATTN_TILE, (b * s) % TM_BACK) == (D_MODEL, 0, 0)
    posc = jnp.repeat(positions.reshape(b, s // SUBLANES, SUBLANES), LANES // SUBLANES, axis=2)
    inv_freq = ROPE_THETA ** (-jnp.arange(0, ROT_DIM, 2, dtype=F32) / ROT_DIM)
    freq = jnp.tile(inv_freq, LANES // (ROT_DIM // 2)).reshape(1, LANES).astype(F32)
    qk_cols = _qk_column_order(od_w_qkv.shape[-1])

    for layer in range(DEPTH):
        i = layer // 2
        if layer % 2 == 0:
            cvec = jnp.stack([ev_conv_b[i], ev_conv_ln_g[i], ev_conv_ln_b[i]])
            svec = jnp.stack([ev_sgu_ln_g[i], ev_sgu_ln_b[i]])
            bias_full = jnp.repeat(ev_b_spatial[i].T, SGU_CH // SGU_GROUPS, axis=1)
            c = _even_front(x, ev_w_in[i].astype(BF16), ev_conv_w[i], cvec, svec,
                            ev_w_spatial[i], bias_full)
            w_out = ev_w_out[i]
        else:
            qkv = _qkv_proj(x, posc, freq, od_w_qkv[i][:, qk_cols].astype(BF16))
            c = _attention(qkv)
            w_out = od_w_out[i]
        ln4 = jnp.stack([ln_g[layer, 0], ln_b[layer, 0], ln_g[layer, 1], ln_b[layer, 1]])
        x = _back(x.reshape(b * s, d), c.reshape(b * s, d), w_out.astype(BF16),
                  ffn_w_gate[layer].astype(BF16), ffn_w_up[layer].astype(BF16),
                  ffn_w_down[layer].astype(BF16), ln4).reshape(b, s, d)
    return x
```

```python
import functools

import jax
import jax.numpy as jnp
from jax import lax
from jax.experimental import pallas as pl
from jax.experimental.pallas import tpu as pltpu

F32 = jnp.float32
BF16 = jnp.bfloat16

D_MODEL = 1024
DEPTH = 4
CONV_CH = 512
CONV_WIDTH = 31
SGU_CH = 512
SGU_GROUPS = 8
SGU_CHUNK = 128
HEADS = 8
HEAD_DIM = 128
ATTN_BLOCK = 128
ATTN_DILATIONS = (1, 4, 16)
ROT_DIM = 32
ROPE_THETA = 500000.0
FFN_HIDDEN = 2816
DN_ALPHA = (2.0 * DEPTH) ** 0.25
LN_EPS = 1e-5
NEG_BIG = -1e30

LANES = 128
SUBLANES = 8
CONV_ROWS = 64
VMEM_LIMIT_BYTES = 56 * 1024 * 1024

TM_BACK = 1024
BACK_ROWS = 256
TM_EVEN = 512
EVEN_ROWS = 128
ATTN_TILE = 2048
N_SLABS = D_MODEL // LANES
R2_LANE = LANES // 2
QKV_ROWS = 512
PERM_BLOCKS_PER_STEP = 6
CONV_HALO = 32


def _layer_norm(v, g, b):
    mu = jnp.mean(v, axis=-1, keepdims=True)
    c = v - mu
    var = jnp.mean(c * c, axis=-1, keepdims=True)
    return c * lax.rsqrt(var + LN_EPS) * g + b


def _resident(shape):
    nd = len(shape)
    return pl.BlockSpec(shape, lambda *_: (0,) * nd, pipeline_mode=pl.Buffered(1))


def _resident_layer(stack, layer):
    nd = stack.ndim - 1
    return pl.BlockSpec((None,) + stack.shape[1:], lambda *_: (layer,) + (0,) * nd,
                        pipeline_mode=pl.Buffered(1))


def _back_kernel(x_ref, c_ref, wo_ref, wg_ref, wu_ref, wd_ref, ln_ref, o_ref):
    rows = [slice(i * BACK_ROWS, (i + 1) * BACK_ROWS) for i in range(TM_BACK // BACK_ROWS)]
    hs = [jnp.dot(c_ref[r, :], wo_ref[...], preferred_element_type=F32) for r in rows]
    x1s = [_layer_norm(DN_ALPHA * x_ref[r, :] + h, ln_ref[0:1, :], ln_ref[1:2, :]) for r, h in zip(rows, hs)]
    acts = []
    for x1 in x1s:
        x1b = x1.astype(BF16)
        g = jnp.dot(x1b, wg_ref[...], preferred_element_type=F32)
        u = jnp.dot(x1b, wu_ref[...], preferred_element_type=F32)
        acts.append((g * jax.nn.sigmoid(g) * u).astype(BF16))
    for r, x1, a in zip(rows, x1s, acts):
        y = jnp.dot(a, wd_ref[...], preferred_element_type=F32)
        o_ref[r, :] = _layer_norm(DN_ALPHA * x1 + y, ln_ref[2:3, :], ln_ref[3:4, :])


def _back(x2d, c2d, w_out, mixer_idx, w_gate, w_up, w_down, layer, ln4):
    n = x2d.shape[0]
    tile = lambda: pl.BlockSpec((TM_BACK, D_MODEL), lambda i: (i, 0))
    return pl.pallas_call(
        _back_kernel,
        grid=(n // TM_BACK,),
        in_specs=[tile(), tile(),
                  _resident_layer(w_out, mixer_idx), _resident_layer(w_gate, layer),
                  _resident_layer(w_up, layer), _resident_layer(w_down, layer),
                  _resident(ln4.shape)],
        out_specs=tile(),
        out_shape=jax.ShapeDtypeStruct((n, D_MODEL), F32),
        compiler_params=pltpu.CompilerParams(
            dimension_semantics=("arbitrary",), vmem_limit_bytes=VMEM_LIMIT_BYTES),
        name="back",
    )(x2d, c2d, w_out, w_gate, w_up, w_down, ln4)


def _even_kernel(x_ref, win_ref, cw_ref, cvec_ref, svec_ref, wsp_ref, bsp_ref, o_ref, abuf_ref):
    tm, rg = TM_EVEN, EVEN_ROWS

    @pl.when(pl.program_id(1) == 0)
    def _():
        abuf_ref[0, 0:CONV_HALO, :] = jnp.zeros((CONV_HALO, CONV_CH), F32)

    row = lax.broadcasted_iota(jnp.int32, (SGU_CHUNK, SGU_CHUNK), 0)
    col = lax.broadcasted_iota(jnp.int32, (SGU_CHUNK, SGU_CHUNK), 1)
    tril = col <= row
    w_pairs = [jnp.concatenate([jnp.where(tril, wsp_ref[2 * p], 0.0), jnp.where(tril, wsp_ref[2 * p + 1], 0.0)],
                               axis=1).astype(BF16) for p in range(SGU_GROUPS // 2)]
    low_half = lax.broadcasted_iota(jnp.int32, (SGU_CHUNK, LANES), 1) < (LANES // 2)
    base = CONV_HALO - (CONV_WIDTH - 1)
    lag = CONV_HALO - SUBLANES

    for r0 in range(0, tm, rg):
        h = jnp.dot(x_ref[0, r0:r0 + rg, :].astype(BF16), win_ref[...], preferred_element_type=F32)

        abuf_ref[0, CONV_HALO + r0:CONV_HALO + r0 + rg, :] = (
            h[:, :CONV_CH] * jax.nn.sigmoid(h[:, CONV_CH:2 * CONV_CH]))
        lo = 0 if r0 == 0 else r0 + lag
        for r in range(1, SUBLANES):
            abuf_ref[r, lo:r0 + rg + lag, :] = abuf_ref[0, lo + r:r0 + rg + lag + r, :]
        for c0 in range(r0, r0 + rg, CONV_ROWS):
            conv = jnp.broadcast_to(cvec_ref[0:1, :], (CONV_ROWS, CONV_CH))
            for j in range(CONV_WIDTH):
                q, r = divmod(base + j, SUBLANES)
                conv = conv + cw_ref[j:j + 1, :] * abuf_ref[r, c0 + SUBLANES * q:c0 + SUBLANES * q + CONV_ROWS, :]
            an = _layer_norm(conv, cvec_ref[1:2, :], cvec_ref[2:3, :])
            o_ref[0, c0:c0 + CONV_ROWS, 0:CONV_CH] = (an * jax.nn.sigmoid(an)).astype(BF16)

        z = h[:, 2 * CONV_CH:]
        z = 0.5 * z * (1.0 + lax.erf(z * (2.0 ** -0.5)))
        u = z[:, :SGU_CH]
        v = _layer_norm(z[:, SGU_CH:], svec_ref[0:1, :], svec_ref[1:2, :])
        for c in range(rg // SGU_CHUNK):
            rows = slice(c * SGU_CHUNK, (c + 1) * SGU_CHUNK)
            for p in range(SGU_GROUPS // 2):
                v_slab = v[rows, p * LANES:(p + 1) * LANES]
                rhs = jnp.concatenate([jnp.where(low_half, v_slab, 0.0).astype(BF16),
                                       jnp.where(low_half, 0.0, v_slab).astype(BF16)], axis=0)
                sv = jnp.dot(w_pairs[p], rhs, preferred_element_type=F32) + bsp_ref[:, p * LANES:(p + 1) * LANES]
                o_ref[0, r0 + c * SGU_CHUNK:r0 + (c + 1) * SGU_CHUNK,
                      CONV_CH + p * LANES:CONV_CH + (p + 1) * LANES] = (
                    u[rows, p * LANES:(p + 1) * LANES] * sv).astype(BF16)

    abuf_ref[0, 0:CONV_HALO, :] = abuf_ref[0, tm:tm + CONV_HALO, :]


def _even_front(x, w_in, layer, conv_w, cvec, svec, w_spatial, bias_full):
    b, s, _ = x.shape
    return pl.pallas_call(
        _even_kernel,
        grid=(b, s // TM_EVEN),
        in_specs=[pl.BlockSpec((1, TM_EVEN, D_MODEL), lambda bi, i: (bi, i, 0)),
                  _resident_layer(w_in, layer), _resident(conv_w.shape), _resident(cvec.shape),
                  _resident(svec.shape), _resident(w_spatial.shape), _resident(bias_full.shape)],
        out_specs=pl.BlockSpec((1, TM_EVEN, D_MODEL), lambda bi, i: (bi, i, 0)),
        out_shape=jax.ShapeDtypeStruct((b, s, D_MODEL), BF16),
        scratch_shapes=[pltpu.VMEM((SUBLANES, CONV_HALO + TM_EVEN, CONV_CH), F32)],
        compiler_params=pltpu.CompilerParams(
            dimension_semantics=("arbitrary", "arbitrary"), vmem_limit_bytes=VMEM_LIMIT_BYTES),
        name="even_front",
    )(x, w_in, conv_w, cvec, svec, w_spatial, bias_full)


def _stream_blocks(dilation):
    if dilation == 1:
        return [(n * ATTN_BLOCK, 1) for n in range(ATTN_TILE // ATTN_BLOCK)]
    if dilation == 4:
        return [(n * ATTN_BLOCK * 4 + rho, 4) for n in range(4) for rho in range(4)]
    return [(rho, 16) for rho in range(16)]


def _qkv_kernel(*refs):
    x_slabs = refs[:N_SLABS]
    posc_ref, freq_ref, w_ref, o_ref = refs[N_SLABS:N_SLABS + 4]
    xp_refs = refs[N_SLABS + 4:N_SLABS + 7]
    cos_ref, sin_ref = refs[N_SLABS + 7:]
    c = pl.program_id(2)
    half = ROT_DIM // 2

    @pl.when(c == 0)
    def _():
        ang = posc_ref[0].astype(F32) * freq_ref[...]
        cos_c, sin_c = jnp.cos(ang), jnp.sin(ang)
        lane_c = lax.broadcasted_iota(jnp.int32, ang.shape, 1)

        def spread(tab, s):
            roll = lambda shift: tab if shift % LANES == 0 else pltpu.roll(tab, shift % LANES, axis=1)
            return roll(LANES - half * s), roll(LANES + R2_LANE - half * s)

        in_r1 = lane_c < half
        in_r2 = (lane_c >= R2_LANE) & (lane_c < R2_LANE + half)
        for s in range(SUBLANES):
            c1, c2 = spread(cos_c, s)
            s1, s2 = spread(sin_c, s)
            rows = pl.ds(s, ATTN_TILE // SUBLANES, stride=SUBLANES)
            cos_ref[0, rows, :] = jnp.where(in_r1, c1, jnp.where(in_r2, c2, 1.0))
            sin_ref[0, rows, :] = jnp.where(in_r1, -s1, jnp.where(in_r2, s2, 0.0))
        for sl in range(N_SLABS):
            xp_refs[0][:, sl * LANES:(sl + 1) * LANES] = x_slabs[sl][0].astype(BF16)
        for gi in (1, 2):
            for blk, (start, stride) in enumerate(_stream_blocks(ATTN_DILATIONS[gi])):
                rows = pl.ds(start, ATTN_BLOCK, stride=stride)
                dst = slice(blk * ATTN_BLOCK, (blk + 1) * ATTN_BLOCK)
                cos_ref[gi, dst, :] = cos_ref[0, rows, :]
                sin_ref[gi, dst, :] = sin_ref[0, rows, :]

    t = c % 3
    scale = jnp.where(t == 0, HEAD_DIM ** -0.5, 1.0).astype(F32)
    rotate = t < 2

    def project(gi):
        def permute_next(i):
            stride = ATTN_DILATIONS[gi + 1]
            blk = jnp.minimum(t * PERM_BLOCKS_PER_STEP + i, ATTN_TILE // ATTN_BLOCK - 1)
            if stride == 4:
                start = (blk >> 2) * (4 * ATTN_BLOCK) + (blk & 3)
            else:
                start = blk
            dst = pl.ds(pl.multiple_of(blk * ATTN_BLOCK, ATTN_BLOCK), ATTN_BLOCK)
            for sl in range(N_SLABS):
                xp_refs[gi + 1][dst, sl * LANES:(sl + 1) * LANES] = (
                    x_slabs[sl][0, pl.ds(start, ATTN_BLOCK, stride=stride), :].astype(BF16))

        n_rc = ATTN_TILE // QKV_ROWS
        for rc in range(n_rc):
            rows = slice(rc * QKV_ROWS, (rc + 1) * QKV_ROWS)
            if gi + 1 < len(ATTN_DILATIONS):
                for i in range(rc, PERM_BLOCKS_PER_STEP, n_rc):
                    permute_next(i)
            y = jnp.dot(xp_refs[gi][rows, :], w_ref[...], preferred_element_type=F32)
            cos = jnp.where(rotate, cos_ref[gi, rows, :], 1.0) * scale
            sin = jnp.where(rotate, sin_ref[gi, rows, :], 0.0) * scale
            for hd in range(HEADS):
                yh = y[:, hd * HEAD_DIM:(hd + 1) * HEAD_DIM]
                o_ref[0, 0, hd, rows, :] = (yh * cos + pltpu.roll(yh, R2_LANE, axis=1) * sin).astype(BF16)

    for gi in range(len(ATTN_DILATIONS)):
        pl.when(c // 3 == gi)(functools.partial(project, gi))


def _qkv_proj(x, posc, freq, w_qkv, layer):
    b, s, _ = x.shape
    n_chunks = w_qkv.shape[2] // D_MODEL
    slab = lambda sl: pl.BlockSpec((1, ATTN_TILE, LANES), lambda bi, ti, c: (bi, ti, sl))
    return pl.pallas_call(
        _qkv_kernel,
        grid=(b, s // ATTN_TILE, n_chunks),
        in_specs=[slab(sl) for sl in range(N_SLABS)] + [
                  pl.BlockSpec((1, ATTN_TILE // SUBLANES, LANES), lambda bi, ti, c: (bi, ti, 0)),
                  pl.BlockSpec((1, LANES), lambda bi, ti, c: (0, 0)),
                  pl.BlockSpec((None, D_MODEL, D_MODEL), lambda bi, ti, c: (layer, 0, c))],
        out_specs=pl.BlockSpec((1, 1, HEADS, ATTN_TILE, HEAD_DIM), lambda bi, ti, c: (c, bi, 0, ti, 0)),
        out_shape=jax.ShapeDtypeStruct((n_chunks, b, HEADS, s, HEAD_DIM), BF16),
        scratch_shapes=[pltpu.VMEM((ATTN_TILE, D_MODEL), BF16)] * 3 + [
                        pltpu.VMEM((3, ATTN_TILE, LANES), F32),
                        pltpu.VMEM((3, ATTN_TILE, LANES), F32)],
        compiler_params=pltpu.CompilerParams(
            dimension_semantics=("arbitrary", "arbitrary", "arbitrary"),
            vmem_limit_bytes=VMEM_LIMIT_BYTES),
        name="qkv_proj",
    )(*([x] * N_SLABS), posc, freq, w_qkv)


def _attn_kernel(q0, k0, v0, q1, k1, v1, q2, k2, v2,
                 pk0, pv0, pk1, pv1, pk2, pv2, o_ref, acc_ref, m_ref, l_ref):
    has_prev = pl.program_id(1) > 0
    qi = lax.broadcasted_iota(jnp.int32, (ATTN_BLOCK, 2 * ATTN_BLOCK), 0)
    kj = lax.broadcasted_iota(jnp.int32, (ATTN_BLOCK, 2 * ATTN_BLOCK), 1)
    band = (kj >= qi) & (kj <= qi + ATTN_BLOCK)
    band_first = band & ((kj >= ATTN_BLOCK) | has_prev)
    contract_last = (((1,), (1,)), ((), ()))
    ones = jnp.ones((2 * ATTN_BLOCK, LANES), BF16)
    qs, ks, vs = (q0, q1, q2), (k0, k1, k2), (v0, v1, v2)
    pks, pvs = (pk0, pk1, pk2), (pv0, pv1, pv2)
    blk_rows = lambda r, i: r[0, 0, 0, i * ATTN_BLOCK:(i + 1) * ATTN_BLOCK, :]

    for gi, dilation in enumerate(ATTN_DILATIONS):
        n_streams = {1: 1, 4: 4, 16: 16}[dilation]
        for blk, (start, stride) in enumerate(_stream_blocks(dilation)):
            if blk < n_streams:
                k_prev, v_prev = blk_rows(pks[gi], blk), blk_rows(pvs[gi], blk)
            else:
                k_prev, v_prev = blk_rows(ks[gi], blk - n_streams), blk_rows(vs[gi], blk - n_streams)
            kcat = jnp.concatenate([k_prev, blk_rows(ks[gi], blk)], axis=0)
            vext = jnp.concatenate(
                [jnp.concatenate([v_prev, blk_rows(vs[gi], blk)], axis=0), ones], axis=1)
            s = lax.dot_general(blk_rows(qs[gi], blk), kcat, contract_last,
                                preferred_element_type=F32)
            s = jnp.where(band_first if blk < n_streams else band, s, NEG_BIG)
            m = jnp.max(s, axis=-1, keepdims=True)
            p = jnp.exp(s - m)
            pvl = jnp.dot(p.astype(BF16), vext, preferred_element_type=F32)
            rows = pl.ds(start, ATTN_BLOCK, stride=stride) if stride > 1 else pl.ds(start, ATTN_BLOCK)
            acc_ref[gi, rows, :] = pvl[:, :HEAD_DIM]
            l_ref[gi, rows, :] = pvl[:, HEAD_DIM:]
            m_ref[gi, rows, :] = jnp.broadcast_to(m, (ATTN_BLOCK, LANES))

    for c in range(ATTN_TILE // ATTN_BLOCK):
        rows = slice(c * ATTN_BLOCK, (c + 1) * ATTN_BLOCK)
        m0, m1, m2 = m_ref[0, rows, :], m_ref[1, rows, :], m_ref[2, rows, :]
        mx = jnp.maximum(jnp.maximum(m0, m1), m2)
        e0, e1, e2 = jnp.exp(m0 - mx), jnp.exp(m1 - mx), jnp.exp(m2 - mx)
        num = e0 * acc_ref[0, rows, :] + e1 * acc_ref[1, rows, :] + e2 * acc_ref[2, rows, :]
        den = e0 * l_ref[0, rows, :] + e1 * l_ref[1, rows, :] + e2 * l_ref[2, rows, :]
        o_ref[0, rows, :] = (num / den).astype(BF16)


def _attention(qkv):
    _, b, _, s, _ = qkv.shape
    n_tiles = s // ATTN_TILE

    def cur(chunk):
        return pl.BlockSpec((1, 1, 1, ATTN_TILE, HEAD_DIM), lambda bi, ti, hd: (chunk, bi, hd, ti, 0))

    def prev(chunk, rows):
        per_tile = ATTN_TILE // rows
        return pl.BlockSpec(
            (1, 1, 1, rows, HEAD_DIM),
            lambda bi, ti, hd: (chunk, bi, hd, jnp.maximum(ti * per_tile - 1, 0), 0))

    prev_rows = (ATTN_BLOCK, 4 * ATTN_BLOCK, ATTN_TILE)
    in_specs = [cur(c) for c in range(9)]
    operands = [qkv] * 9
    for gi in range(3):
        in_specs += [prev(3 * gi + 1, prev_rows[gi]), prev(3 * gi + 2, prev_rows[gi])]
        operands += [qkv, qkv]
    return pl.pallas_call(
        _attn_kernel,
        grid=(b, n_tiles, HEADS),
        in_specs=in_specs,
        out_specs=pl.BlockSpec((1, ATTN_TILE, HEAD_DIM), lambda bi, ti, hd: (bi, ti, hd)),
        out_shape=jax.ShapeDtypeStruct((b, s, HEADS * HEAD_DIM), BF16),
        scratch_shapes=[pltpu.VMEM((3, ATTN_TILE, LANES), F32)] * 3,
        compiler_params=pltpu.CompilerParams(
            dimension_semantics=("arbitrary", "arbitrary", "arbitrary"),
            vmem_limit_bytes=VMEM_LIMIT_BYTES),
        name="attention",
    )(*operands)


def _qk_lane_order(w_qkv):
    half = ROT_DIM // 2
    lead = w_qkv.shape[:-1]
    w = w_qkv.reshape(lead + (-1, 3, HEADS, HEAD_DIM))
    qk = w[..., :2, :, :]
    qk = jnp.concatenate([qk[..., :half], qk[..., ROT_DIM:R2_LANE + half],
                          qk[..., half:ROT_DIM], qk[..., R2_LANE + half:]], axis=-1)
    return jnp.concatenate([qk, w[..., 2:, :, :]], axis=-3).reshape(w_qkv.shape)


def kernel(x, positions, ev_w_in, ev_conv_w, ev_conv_b, ev_conv_ln_g, ev_conv_ln_b,
           ev_sgu_ln_g, ev_sgu_ln_b, ev_w_spatial, ev_b_spatial, ev_w_out,
           od_w_qkv, od_w_out, ffn_w_gate, ffn_w_up, ffn_w_down, ln_g, ln_b):
    b, s, d = x.shape
    assert (d, s % ATTN_TILE, (b * s) % TM_BACK) == (D_MODEL, 0, 0)
    posc = jnp.repeat(positions.reshape(b, s // SUBLANES, SUBLANES), LANES // SUBLANES, axis=2)
    inv_freq = ROPE_THETA ** (-jnp.arange(0, ROT_DIM, 2, dtype=F32) / ROT_DIM)
    freq = jnp.tile(inv_freq, LANES // (ROT_DIM // 2)).reshape(1, LANES).astype(F32)
    w_in, w_qkv = ev_w_in.astype(BF16), _qk_lane_order(od_w_qkv).astype(BF16)
    w_out = (ev_w_out.astype(BF16), od_w_out.astype(BF16))
    w_gate, w_up, w_down = ffn_w_gate.astype(BF16), ffn_w_up.astype(BF16), ffn_w_down.astype(BF16)

    for layer in range(DEPTH):
        i = layer // 2
        if layer % 2 == 0:
            cvec = jnp.stack([ev_conv_b[i], ev_conv_ln_g[i], ev_conv_ln_b[i]])
            svec = jnp.stack([ev_sgu_ln_g[i], ev_sgu_ln_b[i]])
            bias_full = jnp.repeat(ev_b_spatial[i].T, SGU_CH // SGU_GROUPS, axis=1)
            c = _even_front(x, w_in, i, ev_conv_w[i], cvec, svec, ev_w_spatial[i], bias_full)
        else:
            c = _attention(_qkv_proj(x, posc, freq, w_qkv, i))
        ln4 = jnp.stack([ln_g[layer, 0], ln_b[layer, 0], ln_g[layer, 1], ln_b[layer, 1]])
        x = _back(x.reshape(b * s, d), c.reshape(b * s, d), w_out[layer % 2], i,
                  w_gate, w_up, w_down, layer, ln4).reshape(b, s, d)
    return x
```

```python
import functools

import jax
import jax.numpy as jnp
from jax import lax
from jax.experimental import pallas as pl
from jax.experimental.pallas import tpu as pltpu

F32 = jnp.float32
BF16 = jnp.bfloat16

D_MODEL = 1024
DEPTH = 4
CONV_CH = 512
CONV_WIDTH = 31
SGU_CH = 512
SGU_GROUPS = 8
SGU_CHUNK = 128
HEADS = 8
HEAD_DIM = 128
ATTN_BLOCK = 128
ATTN_DILATIONS = (1, 4, 16)
ROT_DIM = 32
ROPE_THETA = 500000.0
FFN_HIDDEN = 2816
DN_ALPHA = (2.0 * DEPTH) ** 0.25
LN_EPS = 1e-5
NEG_BIG = -1e30

LANES = 128
SUBLANES = 8
CONV_ROWS = 64
VMEM_LIMIT_BYTES = 56 * 1024 * 1024

TM_BACK = 1024
BACK_ROWS = 256
TM_EVEN = 1024
EVEN_ROWS = 256
ATTN_TILE = 2048
N_SLABS = D_MODEL // LANES
R2_LANE = LANES // 2
QKV_ROWS = 512
PERM_BLOCKS_PER_STEP = 6
CONV_HALO = 32


def _layer_norm(v, g, b):
    mu = jnp.mean(v, axis=-1, keepdims=True)
    c = v - mu
    var = jnp.mean(c * c, axis=-1, keepdims=True)
    return c * lax.rsqrt(var + LN_EPS) * g + b


def _resident(shape):
    nd = len(shape)
    return pl.BlockSpec(shape, lambda *_: (0,) * nd, pipeline_mode=pl.Buffered(1))


def _resident_layer(stack, layer):
    nd = stack.ndim - 1
    return pl.BlockSpec((None,) + stack.shape[1:], lambda *_: (layer,) + (0,) * nd,
                        pipeline_mode=pl.Buffered(1))


def _back_kernel(x_ref, c_ref, wo_ref, wg_ref, wu_ref, wd_ref, ln_ref, o_ref):
    rows = [slice(i * BACK_ROWS, (i + 1) * BACK_ROWS) for i in range(TM_BACK // BACK_ROWS)]
    hs = [jnp.dot(c_ref[r, :], wo_ref[...], preferred_element_type=F32) for r in rows]
    x1s = [_layer_norm(DN_ALPHA * x_ref[r, :] + h, ln_ref[0:1, :], ln_ref[1:2, :]) for r, h in zip(rows, hs)]
    acts = []
    for x1 in x1s:
        x1b = x1.astype(BF16)
        g = jnp.dot(x1b, wg_ref[...], preferred_element_type=F32)
        u = jnp.dot(x1b, wu_ref[...], preferred_element_type=F32)
        acts.append((g * jax.nn.sigmoid(g) * u).astype(BF16))
    for r, x1, a in zip(rows, x1s, acts):
        y = jnp.dot(a, wd_ref[...], preferred_element_type=F32)
        o_ref[r, :] = _layer_norm(DN_ALPHA * x1 + y, ln_ref[2:3, :], ln_ref[3:4, :])


def _back(x2d, c2d, w_out, mixer_idx, w_gate, w_up, w_down, layer, ln4):
    n = x2d.shape[0]
    tile = lambda: pl.BlockSpec((TM_BACK, D_MODEL), lambda i: (i, 0))
    return pl.pallas_call(
        _back_kernel,
        grid=(n // TM_BACK,),
        in_specs=[tile(), tile(),
                  _resident_layer(w_out, mixer_idx), _resident_layer(w_gate, layer),
                  _resident_layer(w_up, layer), _resident_layer(w_down, layer),
                  _resident(ln4.shape)],
        out_specs=tile(),
        out_shape=jax.ShapeDtypeStruct((n, D_MODEL), F32),
        compiler_params=pltpu.CompilerParams(
            dimension_semantics=("arbitrary",), vmem_limit_bytes=VMEM_LIMIT_BYTES),
        name="back",
    )(x2d, c2d, w_out, w_gate, w_up, w_down, ln4)


def _even_kernel(x_ref, win_ref, cw_ref, cvec_ref, svec_ref, wsp_ref, bsp_ref, o_ref, abuf_ref):
    tm, rg = TM_EVEN, EVEN_ROWS

    @pl.when(pl.program_id(1) == 0)
    def _():
        abuf_ref[0, 0:CONV_HALO, :] = jnp.zeros((CONV_HALO, CONV_CH), F32)

    row = lax.broadcasted_iota(jnp.int32, (SGU_CHUNK, SGU_CHUNK), 0)
    col = lax.broadcasted_iota(jnp.int32, (SGU_CHUNK, SGU_CHUNK), 1)
    tril = col <= row
    w_pairs = [jnp.concatenate([jnp.where(tril, wsp_ref[2 * p], 0.0), jnp.where(tril, wsp_ref[2 * p + 1], 0.0)],
                               axis=1).astype(BF16) for p in range(SGU_GROUPS // 2)]
    low_half = lax.broadcasted_iota(jnp.int32, (SGU_CHUNK, LANES), 1) < (LANES // 2)
    base = CONV_HALO - (CONV_WIDTH - 1)
    lag = CONV_HALO - SUBLANES

    for r0 in range(0, tm, rg):
        h = jnp.dot(x_ref[0, r0:r0 + rg, :].astype(BF16), win_ref[...], preferred_element_type=F32)

        abuf_ref[0, CONV_HALO + r0:CONV_HALO + r0 + rg, :] = (
            h[:, :CONV_CH] * jax.nn.sigmoid(h[:, CONV_CH:2 * CONV_CH]))
        lo = 0 if r0 == 0 else r0 + lag
        for r in range(1, SUBLANES):
            abuf_ref[r, lo:r0 + rg + lag, :] = abuf_ref[0, lo + r:r0 + rg + lag + r, :]
        for c0 in range(r0, r0 + rg, CONV_ROWS):
            conv = jnp.broadcast_to(cvec_ref[0:1, :], (CONV_ROWS, CONV_CH))
            for j in range(CONV_WIDTH):
                q, r = divmod(base + j, SUBLANES)
                conv = conv + cw_ref[j:j + 1, :] * abuf_ref[r, c0 + SUBLANES * q:c0 + SUBLANES * q + CONV_ROWS, :]
            an = _layer_norm(conv, cvec_ref[1:2, :], cvec_ref[2:3, :])
            o_ref[0, c0:c0 + CONV_ROWS, 0:CONV_CH] = (an * jax.nn.sigmoid(an)).astype(BF16)

        z = h[:, 2 * CONV_CH:]
        z = 0.5 * z * (1.0 + lax.erf(z * (2.0 ** -0.5)))
        u = z[:, :SGU_CH]
        v = _layer_norm(z[:, SGU_CH:], svec_ref[0:1, :], svec_ref[1:2, :])
        for c in range(rg // SGU_CHUNK):
            rows = slice(c * SGU_CHUNK, (c + 1) * SGU_CHUNK)
            for p in range(SGU_GROUPS // 2):
                v_slab = v[rows, p * LANES:(p + 1) * LANES]
                rhs = jnp.concatenate([jnp.where(low_half, v_slab, 0.0).astype(BF16),
                                       jnp.where(low_half, 0.0, v_slab).astype(BF16)], axis=0)
                sv = jnp.dot(w_pairs[p], rhs, preferred_element_type=F32) + bsp_ref[:, p * LANES:(p + 1) * LANES]
                o_ref[0, r0 + c * SGU_CHUNK:r0 + (c + 1) * SGU_CHUNK,
                      CONV_CH + p * LANES:CONV_CH + (p + 1) * LANES] = (
                    u[rows, p * LANES:(p + 1) * LANES] * sv).astype(BF16)

    abuf_ref[0, 0:CONV_HALO, :] = abuf_ref[0, tm:tm + CONV_HALO, :]


def _even_front(x, w_in, layer, conv_w, cvec, svec, w_spatial, bias_full):
    b, s, _ = x.shape
    return pl.pallas_call(
        _even_kernel,
        grid=(b, s // TM_EVEN),
        in_specs=[pl.BlockSpec((1, TM_EVEN, D_MODEL), lambda bi, i: (bi, i, 0)),
                  _resident_layer(w_in, layer), _resident(conv_w.shape), _resident(cvec.shape),
                  _resident(svec.shape), _resident(w_spatial.shape), _resident(bias_full.shape)],
        out_specs=pl.BlockSpec((1, TM_EVEN, D_MODEL), lambda bi, i: (bi, i, 0)),
        out_shape=jax.ShapeDtypeStruct((b, s, D_MODEL), BF16),
        scratch_shapes=[pltpu.VMEM((SUBLANES, CONV_HALO + TM_EVEN, CONV_CH), F32)],
        compiler_params=pltpu.CompilerParams(
            dimension_semantics=("arbitrary", "arbitrary"), vmem_limit_bytes=VMEM_LIMIT_BYTES),
        name="even_front",
    )(x, w_in, conv_w, cvec, svec, w_spatial, bias_full)


def _stream_blocks(dilation):
    if dilation == 1:
        return [(n * ATTN_BLOCK, 1) for n in range(ATTN_TILE // ATTN_BLOCK)]
    if dilation == 4:
        return [(n * ATTN_BLOCK * 4 + rho, 4) for n in range(4) for rho in range(4)]
    return [(rho, 16) for rho in range(16)]


def _qkv_kernel(*refs):
    x_slabs = refs[:N_SLABS]
    posc_ref, freq_ref, w_ref, o_ref = refs[N_SLABS:N_SLABS + 4]
    xp_refs = refs[N_SLABS + 4:N_SLABS + 7]
    cos_ref, sin_ref = refs[N_SLABS + 7:]
    c = pl.program_id(2)
    half = ROT_DIM // 2

    @pl.when(c == 0)
    def _():
        ang = posc_ref[0].astype(F32) * freq_ref[...]
        cos_c, sin_c = jnp.cos(ang), jnp.sin(ang)
        lane_c = lax.broadcasted_iota(jnp.int32, ang.shape, 1)

        def spread(tab, s):
            roll = lambda shift: tab if shift % LANES == 0 else pltpu.roll(tab, shift % LANES, axis=1)
            return roll(LANES - half * s), roll(LANES + R2_LANE - half * s)

        in_r1 = lane_c < half
        in_r2 = (lane_c >= R2_LANE) & (lane_c < R2_LANE + half)
        for s in range(SUBLANES):
            c1, c2 = spread(cos_c, s)
            s1, s2 = spread(sin_c, s)
            rows = pl.ds(s, ATTN_TILE // SUBLANES, stride=SUBLANES)
            cos_ref[0, rows, :] = jnp.where(in_r1, c1, jnp.where(in_r2, c2, 1.0))
            sin_ref[0, rows, :] = jnp.where(in_r1, -s1, jnp.where(in_r2, s2, 0.0))
        for sl in range(N_SLABS):
            xp_refs[0][:, sl * LANES:(sl + 1) * LANES] = x_slabs[sl][0].astype(BF16)
        for gi in (1, 2):
            for blk, (start, stride) in enumerate(_stream_blocks(ATTN_DILATIONS[gi])):
                rows = pl.ds(start, ATTN_BLOCK, stride=stride)
                dst = slice(blk * ATTN_BLOCK, (blk + 1) * ATTN_BLOCK)
                cos_ref[gi, dst, :] = cos_ref[0, rows, :]
                sin_ref[gi, dst, :] = sin_ref[0, rows, :]

    t = c % 3
    scale = jnp.where(t == 0, HEAD_DIM ** -0.5, 1.0).astype(F32)
    rotate = t < 2

    def project(gi):
        def permute_next(i):
            stride = ATTN_DILATIONS[gi + 1]
            blk = jnp.minimum(t * PERM_BLOCKS_PER_STEP + i, ATTN_TILE // ATTN_BLOCK - 1)
            if stride == 4:
                start = (blk >> 2) * (4 * ATTN_BLOCK) + (blk & 3)
            else:
                start = blk
            dst = pl.ds(pl.multiple_of(blk * ATTN_BLOCK, ATTN_BLOCK), ATTN_BLOCK)
            for sl in range(N_SLABS):
                xp_refs[gi + 1][dst, sl * LANES:(sl + 1) * LANES] = (
                    x_slabs[sl][0, pl.ds(start, ATTN_BLOCK, stride=stride), :].astype(BF16))

        n_rc = ATTN_TILE // QKV_ROWS
        for rc in range(n_rc):
            rows = slice(rc * QKV_ROWS, (rc + 1) * QKV_ROWS)
            if gi + 1 < len(ATTN_DILATIONS):
                for i in range(rc, PERM_BLOCKS_PER_STEP, n_rc):
                    permute_next(i)
            y = jnp.dot(xp_refs[gi][rows, :], w_ref[...], preferred_element_type=F32)
            cos = jnp.where(rotate, cos_ref[gi, rows, :], 1.0) * scale
            sin = jnp.where(rotate, sin_ref[gi, rows, :], 0.0) * scale
            for hd in range(HEADS):
                yh = y[:, hd * HEAD_DIM:(hd + 1) * HEAD_DIM]
                o_ref[0, 0, hd, rows, :] = (yh * cos + pltpu.roll(yh, R2_LANE, axis=1) * sin).astype(BF16)

    for gi in range(len(ATTN_DILATIONS)):
        pl.when(c // 3 == gi)(functools.partial(project, gi))


def _qkv_proj(x, posc, freq, w_qkv, layer):
    b, s, _ = x.shape
    n_chunks = w_qkv.shape[2] // D_MODEL
    slab = lambda sl: pl.BlockSpec((1, ATTN_TILE, LANES), lambda bi, ti, c: (bi, ti, sl))
    return pl.pallas_call(
        _qkv_kernel,
        grid=(b, s // ATTN_TILE, n_chunks),
        in_specs=[slab(sl) for sl in range(N_SLABS)] + [
                  pl.BlockSpec((1, ATTN_TILE // SUBLANES, LANES), lambda bi, ti, c: (bi, ti, 0)),
                  pl.BlockSpec((1, LANES), lambda bi, ti, c: (0, 0)),
                  pl.BlockSpec((None, D_MODEL, D_MODEL), lambda bi, ti, c: (layer, 0, c))],
        out_specs=pl.BlockSpec((1, 1, HEADS, ATTN_TILE, HEAD_DIM), lambda bi, ti, c: (c, bi, 0, ti, 0)),
        out_shape=jax.ShapeDtypeStruct((n_chunks, b, HEADS, s, HEAD_DIM), BF16),
        scratch_shapes=[pltpu.VMEM((ATTN_TILE, D_MODEL), BF16)] * 3 + [
                        pltpu.VMEM((3, ATTN_TILE, LANES), F32),
                        pltpu.VMEM((3, ATTN_TILE, LANES), F32)],
        compiler_params=pltpu.CompilerParams(
            dimension_semantics=("arbitrary", "arbitrary", "arbitrary"),
            vmem_limit_bytes=VMEM_LIMIT_BYTES),
        name="qkv_proj",
    )(*([x] * N_SLABS), posc, freq, w_qkv)


def _attn_kernel(q0, k0, v0, q1, k1, v1, q2, k2, v2,
                 pk0, pv0, pk1, pv1, pk2, pv2, o_ref, acc_ref, m_ref, l_ref):
    has_prev = pl.program_id(1) > 0
    qi = lax.broadcasted_iota(jnp.int32, (ATTN_BLOCK, 2 * ATTN_BLOCK), 0)
    kj = lax.broadcasted_iota(jnp.int32, (ATTN_BLOCK, 2 * ATTN_BLOCK), 1)
    band = (kj >= qi) & (kj <= qi + ATTN_BLOCK)
    band_first = band & ((kj >= ATTN_BLOCK) | has_prev)
    contract_last = (((1,), (1,)), ((), ()))
    ones = jnp.ones((2 * ATTN_BLOCK, LANES), BF16)
    qs, ks, vs = (q0, q1, q2), (k0, k1, k2), (v0, v1, v2)
    pks, pvs = (pk0, pk1, pk2), (pv0, pv1, pv2)
    blk_rows = lambda r, i: r[0, 0, 0, i * ATTN_BLOCK:(i + 1) * ATTN_BLOCK, :]

    for gi, dilation in enumerate(ATTN_DILATIONS):
        n_streams = {1: 1, 4: 4, 16: 16}[dilation]
        for blk, (start, stride) in enumerate(_stream_blocks(dilation)):
            if blk < n_streams:
                k_prev, v_prev = blk_rows(pks[gi], blk), blk_rows(pvs[gi], blk)
            else:
                k_prev, v_prev = blk_rows(ks[gi], blk - n_streams), blk_rows(vs[gi], blk - n_streams)
            kcat = jnp.concatenate([k_prev, blk_rows(ks[gi], blk)], axis=0)
            vext = jnp.concatenate(
                [jnp.concatenate([v_prev, blk_rows(vs[gi], blk)], axis=0), ones], axis=1)
            s = lax.dot_general(blk_rows(qs[gi], blk), kcat, contract_last,
                                preferred_element_type=F32)
            s = jnp.where(band_first if blk < n_streams else band, s, NEG_BIG)
            m = jnp.max(s, axis=-1, keepdims=True)
            p = jnp.exp(s - m)
            pvl = jnp.dot(p.astype(BF16), vext, preferred_element_type=F32)
            rows = pl.ds(start, ATTN_BLOCK, stride=stride) if stride > 1 else pl.ds(start, ATTN_BLOCK)
            acc_ref[gi, rows, :] = pvl[:, :HEAD_DIM]
            l_ref[gi, rows, :] = pvl[:, HEAD_DIM:]
            m_ref[gi, rows, :] = jnp.broadcast_to(m, (ATTN_BLOCK, LANES))

    for c in range(ATTN_TILE // ATTN_BLOCK):
        rows = slice(c * ATTN_BLOCK, (c + 1) * ATTN_BLOCK)
        m0, m1, m2 = m_ref[0, rows, :], m_ref[1, rows, :], m_ref[2, rows, :]
        mx = jnp.maximum(jnp.maximum(m0, m1), m2)
        e0, e1, e2 = jnp.exp(m0 - mx), jnp.exp(m1 - mx), jnp.exp(m2 - mx)
        num = e0 * acc_ref[0, rows, :] + e1 * acc_ref[1, rows, :] + e2 * acc_ref[2, rows, :]
        den = e0 * l_ref[0, rows, :] + e1 * l_ref[1, rows, :] + e2 * l_ref[2, rows, :]
        o_ref[0, rows, :] = (num / den).astype(BF16)


def _attention(qkv):
    _, b, _, s, _ = qkv.shape
    n_tiles = s // ATTN_TILE

    def cur(chunk):
        return pl.BlockSpec((1, 1, 1, ATTN_TILE, HEAD_DIM), lambda bi, ti, hd: (chunk, bi, hd, ti, 0))

    def prev(chunk, rows):
        per_tile = ATTN_TILE // rows
        return pl.BlockSpec(
            (1, 1, 1, rows, HEAD_DIM),
            lambda bi, ti, hd: (chunk, bi, hd, jnp.maximum(ti * per_tile - 1, 0), 0))

    prev_rows = (ATTN_BLOCK, 4 * ATTN_BLOCK, ATTN_TILE)
    in_specs = [cur(c) for c in range(9)]
    operands = [qkv] * 9
    for gi in range(3):
        in_specs += [prev(3 * gi + 1, prev_rows[gi]), prev(3 * gi + 2, prev_rows[gi])]
        operands += [qkv, qkv]
    return pl.pallas_call(
        _attn_kernel,
        grid=(b, n_tiles, HEADS),
        in_specs=in_specs,
        out_specs=pl.BlockSpec((1, ATTN_TILE, HEAD_DIM), lambda bi, ti, hd: (bi, ti, hd)),
        out_shape=jax.ShapeDtypeStruct((b, s, HEADS * HEAD_DIM), BF16),
        scratch_shapes=[pltpu.VMEM((3, ATTN_TILE, LANES), F32)] * 3,
        compiler_params=pltpu.CompilerParams(
            dimension_semantics=("arbitrary", "arbitrary", "arbitrary"),
            vmem_limit_bytes=VMEM_LIMIT_BYTES),
        name="attention",
    )(*operands)


def _qkv_weight_kernel(w_ref, o_ref):
    half = ROT_DIM // 2
    is_qk = pl.program_id(1) % 3 < 2
    lane = lax.broadcasted_iota(jnp.int32, (D_MODEL, HEAD_DIM), 1)
    keep = jnp.logical_not(is_qk) | (lane < half) | (lane >= R2_LANE + half)
    for hd in range(HEADS):
        w = w_ref[:, hd * HEAD_DIM:(hd + 1) * HEAD_DIM]
        moved = jnp.where(lane < R2_LANE, pltpu.roll(w, HEAD_DIM - half, axis=1),
                          pltpu.roll(w, R2_LANE - half, axis=1))
        o_ref[:, hd * HEAD_DIM:(hd + 1) * HEAD_DIM] = jnp.where(keep, w, moved).astype(BF16)


def _qkv_weights(w_qkv):
    n_layers, d, n_cols = w_qkv.shape
    block = lambda: pl.BlockSpec((None, d, D_MODEL), lambda li, c: (li, 0, c))
    return pl.pallas_call(
        _qkv_weight_kernel,
        grid=(n_layers, n_cols // D_MODEL),
        in_specs=[block()],
        out_specs=block(),
        out_shape=jax.ShapeDtypeStruct(w_qkv.shape, BF16),
        compiler_params=pltpu.CompilerParams(
            dimension_semantics=("arbitrary", "arbitrary"), vmem_limit_bytes=VMEM_LIMIT_BYTES),
        name="qkv_weights",
    )(w_qkv)


def kernel(x, positions, ev_w_in, ev_conv_w, ev_conv_b, ev_conv_ln_g, ev_conv_ln_b,
           ev_sgu_ln_g, ev_sgu_ln_b, ev_w_spatial, ev_b_spatial, ev_w_out,
           od_w_qkv, od_w_out, ffn_w_gate, ffn_w_up, ffn_w_down, ln_g, ln_b):
    b, s, d = x.shape
    assert (d, s % ATTN_TILE, (b * s) % TM_BACK) == (D_MODEL, 0, 0)
    posc = jnp.repeat(positions.reshape(b, s // SUBLANES, SUBLANES), LANES // SUBLANES, axis=2)
    inv_freq = ROPE_THETA ** (-jnp.arange(0, ROT_DIM, 2, dtype=F32) / ROT_DIM)
    freq = jnp.tile(inv_freq, LANES // (ROT_DIM // 2)).reshape(1, LANES).astype(F32)
    w_in, w_qkv = ev_w_in.astype(BF16), _qkv_weights(od_w_qkv)
    w_out = (ev_w_out.astype(BF16), od_w_out.astype(BF16))
    w_gate, w_up, w_down = ffn_w_gate.astype(BF16), ffn_w_up.astype(BF16), ffn_w_down.astype(BF16)

    for layer in range(DEPTH):
        i = layer // 2
        if layer % 2 == 0:
            cvec = jnp.stack([ev_conv_b[i], ev_conv_ln_g[i], ev_conv_ln_b[i]])
            svec = jnp.stack([ev_sgu_ln_g[i], ev_sgu_ln_b[i]])
            bias_full = jnp.repeat(ev_b_spatial[i].T, SGU_CH // SGU_GROUPS, axis=1)
            c = _even_front(x, w_in, i, ev_conv_w[i], cvec, svec, ev_w_spatial[i], bias_full)
        else:
            c = _attention(_qkv_proj(x, posc, freq, w_qkv, i))
        ln4 = jnp.stack([ln_g[layer, 0], ln_b[layer, 0], ln_g[layer, 1], ln_b[layer, 1]])
        x = _back(x.reshape(b * s, d), c.reshape(b * s, d), w_out[layer % 2], i,
                  w_gate, w_up, w_down, layer, ln4).reshape(b, s, d)
    return x
```

```python
import functools

import jax
import jax.numpy as jnp
from jax import lax
from jax.experimental import pallas as pl
from jax.experimental.pallas import tpu as pltpu

F32 = jnp.float32
BF16 = jnp.bfloat16

D_MODEL = 1024
DEPTH = 4
CONV_CH = 512
CONV_WIDTH = 31
SGU_CH = 512
SGU_GROUPS = 8
SGU_CHUNK = 128
HEADS = 8
HEAD_DIM = 128
ATTN_BLOCK = 128
ATTN_DILATIONS = (1, 4, 16)
ROT_DIM = 32
ROPE_THETA = 500000.0
FFN_HIDDEN = 2816
DN_ALPHA = (2.0 * DEPTH) ** 0.25
LN_EPS = 1e-5
NEG_BIG = -1e30

LANES = 128
SUBLANES = 8
CONV_ROWS = 64
VMEM_LIMIT_BYTES = 56 * 1024 * 1024

TM_BACK = 1024
BACK_ROWS = 256
TM_EVEN = 1024
EVEN_ROWS = 256
ATTN_TILE = 2048
ATTN_HEADS_PER_STEP = 2
N_SLABS = D_MODEL // LANES
R2_LANE = LANES // 2
QKV_ROWS = 512
PERM_BLOCKS_PER_STEP = 6
CONV_HALO = 32


def _layer_norm(v, g, b):
    mu = jnp.mean(v, axis=-1, keepdims=True)
    c = v - mu
    var = jnp.mean(c * c, axis=-1, keepdims=True)
    return c * lax.rsqrt(var + LN_EPS) * g + b


def _resident(shape):
    nd = len(shape)
    return pl.BlockSpec(shape, lambda *_: (0,) * nd, pipeline_mode=pl.Buffered(1))


def _resident_layer(stack, layer):
    nd = stack.ndim - 1
    return pl.BlockSpec((None,) + stack.shape[1:], lambda *_: (layer,) + (0,) * nd,
                        pipeline_mode=pl.Buffered(1))


def _back_kernel(x_ref, c_ref, wo_ref, wg_ref, wu_ref, wd_ref, ln_ref, o_ref):
    rows = [slice(i * BACK_ROWS, (i + 1) * BACK_ROWS) for i in range(TM_BACK // BACK_ROWS)]
    hs = [jnp.dot(c_ref[r, :], wo_ref[...], preferred_element_type=F32) for r in rows]
    x1s = [_layer_norm(DN_ALPHA * x_ref[r, :] + h, ln_ref[0:1, :], ln_ref[1:2, :]) for r, h in zip(rows, hs)]
    acts = []
    for x1 in x1s:
        x1b = x1.astype(BF16)
        g = jnp.dot(x1b, wg_ref[...], preferred_element_type=F32)
        u = jnp.dot(x1b, wu_ref[...], preferred_element_type=F32)
        acts.append((g * jax.nn.sigmoid(g) * u).astype(BF16))
    for r, x1, a in zip(rows, x1s, acts):
        y = jnp.dot(a, wd_ref[...], preferred_element_type=F32)
        o_ref[r, :] = _layer_norm(DN_ALPHA * x1 + y, ln_ref[2:3, :], ln_ref[3:4, :])


def _back(x2d, c2d, w_out, mixer_idx, w_gate, w_up, w_down, layer, ln4):
    n = x2d.shape[0]
    tile = lambda: pl.BlockSpec((TM_BACK, D_MODEL), lambda i: (i, 0))
    return pl.pallas_call(
        _back_kernel,
        grid=(n // TM_BACK,),
        in_specs=[tile(), tile(),
                  _resident_layer(w_out, mixer_idx), _resident_layer(w_gate, layer),
                  _resident_layer(w_up, layer), _resident_layer(w_down, layer),
                  _resident(ln4.shape)],
        out_specs=tile(),
        out_shape=jax.ShapeDtypeStruct((n, D_MODEL), F32),
        compiler_params=pltpu.CompilerParams(
            dimension_semantics=("arbitrary",), vmem_limit_bytes=VMEM_LIMIT_BYTES),
        name="back",
    )(x2d, c2d, w_out, w_gate, w_up, w_down, ln4)


def _even_kernel(x_ref, win_ref, cw_ref, cvec_ref, svec_ref, wsp_ref, bsp_ref, o_ref, abuf_ref):
    tm, rg = TM_EVEN, EVEN_ROWS

    @pl.when(pl.program_id(1) == 0)
    def _():
        abuf_ref[0, 0:CONV_HALO, :] = jnp.zeros((CONV_HALO, CONV_CH), F32)

    row = lax.broadcasted_iota(jnp.int32, (SGU_CHUNK, SGU_CHUNK), 0)
    col = lax.broadcasted_iota(jnp.int32, (SGU_CHUNK, SGU_CHUNK), 1)
    tril = col <= row
    w_pairs = [jnp.concatenate([jnp.where(tril, wsp_ref[2 * p], 0.0), jnp.where(tril, wsp_ref[2 * p + 1], 0.0)],
                               axis=1).astype(BF16) for p in range(SGU_GROUPS // 2)]
    low_half = lax.broadcasted_iota(jnp.int32, (SGU_CHUNK, LANES), 1) < (LANES // 2)
    base = CONV_HALO - (CONV_WIDTH - 1)
    lag = CONV_HALO - SUBLANES

    for r0 in range(0, tm, rg):
        h = jnp.dot(x_ref[0, r0:r0 + rg, :].astype(BF16), win_ref[...], preferred_element_type=F32)

        abuf_ref[0, CONV_HALO + r0:CONV_HALO + r0 + rg, :] = (
            h[:, :CONV_CH] * jax.nn.sigmoid(h[:, CONV_CH:2 * CONV_CH]))
        lo = 0 if r0 == 0 else r0 + lag
        for r in range(1, SUBLANES):
            abuf_ref[r, lo:r0 + rg + lag, :] = abuf_ref[0, lo + r:r0 + rg + lag + r, :]
        for c0 in range(r0, r0 + rg, CONV_ROWS):
            conv = jnp.broadcast_to(cvec_ref[0:1, :], (CONV_ROWS, CONV_CH))
            for j in range(CONV_WIDTH):
                q, r = divmod(base + j, SUBLANES)
                conv = conv + cw_ref[j:j + 1, :] * abuf_ref[r, c0 + SUBLANES * q:c0 + SUBLANES * q + CONV_ROWS, :]
            an = _layer_norm(conv, cvec_ref[1:2, :], cvec_ref[2:3, :])
            o_ref[0, c0:c0 + CONV_ROWS, 0:CONV_CH] = (an * jax.nn.sigmoid(an)).astype(BF16)

        z = h[:, 2 * CONV_CH:]
        z = 0.5 * z * (1.0 + lax.erf(z * (2.0 ** -0.5)))
        u = z[:, :SGU_CH]
        v = _layer_norm(z[:, SGU_CH:], svec_ref[0:1, :], svec_ref[1:2, :])
        for c in range(rg // SGU_CHUNK):
            rows = slice(c * SGU_CHUNK, (c + 1) * SGU_CHUNK)
            for p in range(SGU_GROUPS // 2):
                v_slab = v[rows, p * LANES:(p + 1) * LANES]
                rhs = jnp.concatenate([jnp.where(low_half, v_slab, 0.0).astype(BF16),
                                       jnp.where(low_half, 0.0, v_slab).astype(BF16)], axis=0)
                sv = jnp.dot(w_pairs[p], rhs, preferred_element_type=F32) + bsp_ref[:, p * LANES:(p + 1) * LANES]
                o_ref[0, r0 + c * SGU_CHUNK:r0 + (c + 1) * SGU_CHUNK,
                      CONV_CH + p * LANES:CONV_CH + (p + 1) * LANES] = (
                    u[rows, p * LANES:(p + 1) * LANES] * sv).astype(BF16)

    abuf_ref[0, 0:CONV_HALO, :] = abuf_ref[0, tm:tm + CONV_HALO, :]


def _even_front(x, w_in, layer, conv_w, cvec, svec, w_spatial, bias_full):
    b, s, _ = x.shape
    return pl.pallas_call(
        _even_kernel,
        grid=(b, s // TM_EVEN),
        in_specs=[pl.BlockSpec((1, TM_EVEN, D_MODEL), lambda bi, i: (bi, i, 0)),
                  _resident_layer(w_in, layer), _resident(conv_w.shape), _resident(cvec.shape),
                  _resident(svec.shape), _resident(w_spatial.shape), _resident(bias_full.shape)],
        out_specs=pl.BlockSpec((1, TM_EVEN, D_MODEL), lambda bi, i: (bi, i, 0)),
        out_shape=jax.ShapeDtypeStruct((b, s, D_MODEL), BF16),
        scratch_shapes=[pltpu.VMEM((SUBLANES, CONV_HALO + TM_EVEN, CONV_CH), F32)],
        compiler_params=pltpu.CompilerParams(
            dimension_semantics=("arbitrary", "arbitrary"), vmem_limit_bytes=VMEM_LIMIT_BYTES),
        name="even_front",
    )(x, w_in, conv_w, cvec, svec, w_spatial, bias_full)


def _stream_blocks(dilation):
    if dilation == 1:
        return [(n * ATTN_BLOCK, 1) for n in range(ATTN_TILE // ATTN_BLOCK)]
    if dilation == 4:
        return [(n * ATTN_BLOCK * 4 + rho, 4) for n in range(4) for rho in range(4)]
    return [(rho, 16) for rho in range(16)]


def _qkv_kernel(*refs):
    x_slabs = refs[:N_SLABS]
    posc_ref, freq_ref, w_ref, o_ref = refs[N_SLABS:N_SLABS + 4]
    xp_refs = refs[N_SLABS + 4:N_SLABS + 7]
    cos_ref, sin_ref = refs[N_SLABS + 7:]
    c = pl.program_id(2)
    half = ROT_DIM // 2

    @pl.when(c == 0)
    def _():
        ang = posc_ref[0].astype(F32) * freq_ref[...]
        cos_c, sin_c = jnp.cos(ang), jnp.sin(ang)
        lane_c = lax.broadcasted_iota(jnp.int32, ang.shape, 1)

        def spread(tab, s):
            roll = lambda shift: tab if shift % LANES == 0 else pltpu.roll(tab, shift % LANES, axis=1)
            return roll(LANES - half * s), roll(LANES + R2_LANE - half * s)

        in_r1 = lane_c < half
        in_r2 = (lane_c >= R2_LANE) & (lane_c < R2_LANE + half)
        for s in range(SUBLANES):
            c1, c2 = spread(cos_c, s)
            s1, s2 = spread(sin_c, s)
            rows = pl.ds(s, ATTN_TILE // SUBLANES, stride=SUBLANES)
            cos_ref[0, rows, :] = jnp.where(in_r1, c1, jnp.where(in_r2, c2, 1.0))
            sin_ref[0, rows, :] = jnp.where(in_r1, -s1, jnp.where(in_r2, s2, 0.0))
        for sl in range(N_SLABS):
            xp_refs[0][:, sl * LANES:(sl + 1) * LANES] = x_slabs[sl][0].astype(BF16)
        for gi in (1, 2):
            for blk, (start, stride) in enumerate(_stream_blocks(ATTN_DILATIONS[gi])):
                rows = pl.ds(start, ATTN_BLOCK, stride=stride)
                dst = slice(blk * ATTN_BLOCK, (blk + 1) * ATTN_BLOCK)
                cos_ref[gi, dst, :] = cos_ref[0, rows, :]
                sin_ref[gi, dst, :] = sin_ref[0, rows, :]

    t = c % 3
    scale = jnp.where(t == 0, HEAD_DIM ** -0.5, 1.0).astype(F32)
    rotate = t < 2

    def project(gi):
        def permute_next(i):
            stride = ATTN_DILATIONS[gi + 1]
            blk = jnp.minimum(t * PERM_BLOCKS_PER_STEP + i, ATTN_TILE // ATTN_BLOCK - 1)
            if stride == 4:
                start = (blk >> 2) * (4 * ATTN_BLOCK) + (blk & 3)
            else:
                start = blk
            dst = pl.ds(pl.multiple_of(blk * ATTN_BLOCK, ATTN_BLOCK), ATTN_BLOCK)
            for sl in range(N_SLABS):
                xp_refs[gi + 1][dst, sl * LANES:(sl + 1) * LANES] = (
                    x_slabs[sl][0, pl.ds(start, ATTN_BLOCK, stride=stride), :].astype(BF16))

        n_rc = ATTN_TILE // QKV_ROWS
        for rc in range(n_rc):
            rows = slice(rc * QKV_ROWS, (rc + 1) * QKV_ROWS)
            if gi + 1 < len(ATTN_DILATIONS):
                for i in range(rc, PERM_BLOCKS_PER_STEP, n_rc):
                    permute_next(i)
            y = jnp.dot(xp_refs[gi][rows, :], w_ref[...], preferred_element_type=F32)
            cos = jnp.where(rotate, cos_ref[gi, rows, :], 1.0) * scale
            sin = jnp.where(rotate, sin_ref[gi, rows, :], 0.0) * scale
            for hd in range(HEADS):
                yh = y[:, hd * HEAD_DIM:(hd + 1) * HEAD_DIM]
                o_ref[0, 0, hd, rows, :] = (yh * cos + pltpu.roll(yh, R2_LANE, axis=1) * sin).astype(BF16)

    for gi in range(len(ATTN_DILATIONS)):
        pl.when(c // 3 == gi)(functools.partial(project, gi))


def _qkv_proj(x, posc, freq, w_qkv, layer):
    b, s, _ = x.shape
    n_chunks = w_qkv.shape[2] // D_MODEL
    slab = lambda sl: pl.BlockSpec((1, ATTN_TILE, LANES), lambda bi, ti, c: (bi, ti, sl))
    return pl.pallas_call(
        _qkv_kernel,
        grid=(b, s // ATTN_TILE, n_chunks),
        in_specs=[slab(sl) for sl in range(N_SLABS)] + [
                  pl.BlockSpec((1, ATTN_TILE // SUBLANES, LANES), lambda bi, ti, c: (bi, ti, 0)),
                  pl.BlockSpec((1, LANES), lambda bi, ti, c: (0, 0)),
                  pl.BlockSpec((None, D_MODEL, D_MODEL), lambda bi, ti, c: (layer, 0, c))],
        out_specs=pl.BlockSpec((1, 1, HEADS, ATTN_TILE, HEAD_DIM), lambda bi, ti, c: (c, bi, 0, ti, 0)),
        out_shape=jax.ShapeDtypeStruct((n_chunks, b, HEADS, s, HEAD_DIM), BF16),
        scratch_shapes=[pltpu.VMEM((ATTN_TILE, D_MODEL), BF16)] * 3 + [
                        pltpu.VMEM((3, ATTN_TILE, LANES), F32),
                        pltpu.VMEM((3, ATTN_TILE, LANES), F32)],
        compiler_params=pltpu.CompilerParams(
            dimension_semantics=("arbitrary", "arbitrary", "arbitrary"),
            vmem_limit_bytes=VMEM_LIMIT_BYTES),
        name="qkv_proj",
    )(*([x] * N_SLABS), posc, freq, w_qkv)


def _attn_kernel(q0, k0, v0, q1, k1, v1, q2, k2, v2,
                 pk0, pv0, pk1, pv1, pk2, pv2, o_ref, acc_ref, m_ref, l_ref):
    has_prev = pl.program_id(1) > 0
    qi = lax.broadcasted_iota(jnp.int32, (ATTN_BLOCK, 2 * ATTN_BLOCK), 0)
    kj = lax.broadcasted_iota(jnp.int32, (ATTN_BLOCK, 2 * ATTN_BLOCK), 1)
    band = (kj >= qi) & (kj <= qi + ATTN_BLOCK)
    band_first = band & ((kj >= ATTN_BLOCK) | has_prev)
    ones = jnp.ones((2 * ATTN_BLOCK, LANES), BF16)
    qs, ks, vs = (q0, q1, q2), (k0, k1, k2), (v0, v1, v2)
    pks, pvs = (pk0, pk1, pk2), (pv0, pv1, pv2)
    for hh in range(ATTN_HEADS_PER_STEP):
        blk_rows = lambda r, i, hh=hh: r[0, 0, hh, i * ATTN_BLOCK:(i + 1) * ATTN_BLOCK, :]
        g0 = hh * len(ATTN_DILATIONS)
        _attn_head(qs, ks, vs, pks, pvs, blk_rows, band, band_first, ones, acc_ref, m_ref, l_ref, g0)
        _merge_groups(acc_ref, m_ref, l_ref, g0, o_ref, hh)


def _merge_groups(acc_ref, m_ref, l_ref, g0, o_ref, hh):
    for c in range(ATTN_TILE // ATTN_BLOCK):
        rows = slice(c * ATTN_BLOCK, (c + 1) * ATTN_BLOCK)
        m0, m1, m2 = m_ref[g0, rows, :], m_ref[g0 + 1, rows, :], m_ref[g0 + 2, rows, :]
        mx = jnp.maximum(jnp.maximum(m0, m1), m2)
        e0, e1, e2 = jnp.exp(m0 - mx), jnp.exp(m1 - mx), jnp.exp(m2 - mx)
        num = e0 * acc_ref[g0, rows, :] + e1 * acc_ref[g0 + 1, rows, :] + e2 * acc_ref[g0 + 2, rows, :]
        den = e0 * l_ref[g0, rows, :] + e1 * l_ref[g0 + 1, rows, :] + e2 * l_ref[g0 + 2, rows, :]
        o_ref[0, rows, hh * HEAD_DIM:(hh + 1) * HEAD_DIM] = (num / den).astype(BF16)


def _attn_head(qs, ks, vs, pks, pvs, blk_rows, band, band_first, ones, acc_ref, m_ref, l_ref, g0):
    contract_last = (((1,), (1,)), ((), ()))
    for gi, dilation in enumerate(ATTN_DILATIONS):
        n_streams = {1: 1, 4: 4, 16: 16}[dilation]
        for blk, (start, stride) in enumerate(_stream_blocks(dilation)):
            if blk < n_streams:
                k_prev, v_prev = blk_rows(pks[gi], blk), blk_rows(pvs[gi], blk)
            else:
                k_prev, v_prev = blk_rows(ks[gi], blk - n_streams), blk_rows(vs[gi], blk - n_streams)
            kcat = jnp.concatenate([k_prev, blk_rows(ks[gi], blk)], axis=0)
            vext = jnp.concatenate(
                [jnp.concatenate([v_prev, blk_rows(vs[gi], blk)], axis=0), ones], axis=1)
            s = lax.dot_general(blk_rows(qs[gi], blk), kcat, contract_last,
                                preferred_element_type=F32)
            s = jnp.where(band_first if blk < n_streams else band, s, NEG_BIG)
            m = jnp.max(s, axis=-1, keepdims=True)
            p = jnp.exp(s - m)
            pvl = jnp.dot(p.astype(BF16), vext, preferred_element_type=F32)
            rows = pl.ds(start, ATTN_BLOCK, stride=stride) if stride > 1 else pl.ds(start, ATTN_BLOCK)
            acc_ref[g0 + gi, rows, :] = pvl[:, :HEAD_DIM]
            l_ref[g0 + gi, rows, :] = pvl[:, HEAD_DIM:]
            m_ref[g0 + gi, rows, :] = jnp.broadcast_to(m, (ATTN_BLOCK, LANES))


def _attention(qkv):
    _, b, _, s, _ = qkv.shape
    n_tiles = s // ATTN_TILE

    hps = ATTN_HEADS_PER_STEP

    def cur(chunk):
        return pl.BlockSpec((1, 1, hps, ATTN_TILE, HEAD_DIM), lambda bi, ti, hd: (chunk, bi, hd, ti, 0))

    def prev(chunk, rows):
        per_tile = ATTN_TILE // rows
        return pl.BlockSpec(
            (1, 1, hps, rows, HEAD_DIM),
            lambda bi, ti, hd: (chunk, bi, hd, jnp.maximum(ti * per_tile - 1, 0), 0))

    prev_rows = (ATTN_BLOCK, 4 * ATTN_BLOCK, ATTN_TILE)
    in_specs = [cur(c) for c in range(9)]
    operands = [qkv] * 9
    for gi in range(3):
        in_specs += [prev(3 * gi + 1, prev_rows[gi]), prev(3 * gi + 2, prev_rows[gi])]
        operands += [qkv, qkv]
    return pl.pallas_call(
        _attn_kernel,
        grid=(b, n_tiles, HEADS // hps),
        in_specs=in_specs,
        out_specs=pl.BlockSpec((1, ATTN_TILE, hps * HEAD_DIM), lambda bi, ti, hd: (bi, ti, hd)),
        out_shape=jax.ShapeDtypeStruct((b, s, HEADS * HEAD_DIM), BF16),
        scratch_shapes=[pltpu.VMEM((hps * len(ATTN_DILATIONS), ATTN_TILE, LANES), F32)] * 3,
        compiler_params=pltpu.CompilerParams(
            dimension_semantics=("arbitrary", "arbitrary", "arbitrary"),
            vmem_limit_bytes=VMEM_LIMIT_BYTES),
        name="attention",
    )(*operands)


def _qkv_weight_kernel(w_ref, o_ref):
    half = ROT_DIM // 2
    is_qk = pl.program_id(1) % 3 < 2
    lane = lax.broadcasted_iota(jnp.int32, (D_MODEL, HEAD_DIM), 1)
    keep = jnp.logical_not(is_qk) | (lane < half) | (lane >= R2_LANE + half)
    for hd in range(HEADS):
        w = w_ref[:, hd * HEAD_DIM:(hd + 1) * HEAD_DIM]
        moved = jnp.where(lane < R2_LANE, pltpu.roll(w, HEAD_DIM - half, axis=1),
                          pltpu.roll(w, R2_LANE - half, axis=1))
        o_ref[:, hd * HEAD_DIM:(hd + 1) * HEAD_DIM] = jnp.where(keep, w, moved).astype(BF16)


def _qkv_weights(w_qkv):
    n_layers, d, n_cols = w_qkv.shape
    block = lambda: pl.BlockSpec((None, d, D_MODEL), lambda li, c: (li, 0, c))
    return pl.pallas_call(
        _qkv_weight_kernel,
        grid=(n_layers, n_cols // D_MODEL),
        in_specs=[block()],
        out_specs=block(),
        out_shape=jax.ShapeDtypeStruct(w_qkv.shape, BF16),
        compiler_params=pltpu.CompilerParams(
            dimension_semantics=("arbitrary", "arbitrary"), vmem_limit_bytes=VMEM_LIMIT_BYTES),
        name="qkv_weights",
    )(w_qkv)


def kernel(x, positions, ev_w_in, ev_conv_w, ev_conv_b, ev_conv_ln_g, ev_conv_ln_b,
           ev_sgu_ln_g, ev_sgu_ln_b, ev_w_spatial, ev_b_spatial, ev_w_out,
           od_w_qkv, od_w_out, ffn_w_gate, ffn_w_up, ffn_w_down, ln_g, ln_b):
    b, s, d = x.shape
    assert (d, s % ATTN_TILE, (b * s) % TM_BACK) == (D_MODEL, 0, 0)
    posc = jnp.repeat(positions.reshape(b, s // SUBLANES, SUBLANES), LANES // SUBLANES, axis=2)
    inv_freq = ROPE_THETA ** (-jnp.arange(0, ROT_DIM, 2, dtype=F32) / ROT_DIM)
    freq = jnp.tile(inv_freq, LANES // (ROT_DIM // 2)).reshape(1, LANES).astype(F32)
    w_in, w_qkv = ev_w_in.astype(BF16), _qkv_weights(od_w_qkv)
    w_out = (ev_w_out.astype(BF16), od_w_out.astype(BF16))
    w_gate, w_up, w_down = ffn_w_gate.astype(BF16), ffn_w_up.astype(BF16), ffn_w_down.astype(BF16)

    for layer in range(DEPTH):
        i = layer // 2
        if layer % 2 == 0:
            cvec = jnp.stack([ev_conv_b[i], ev_conv_ln_g[i], ev_conv_ln_b[i]])
            svec = jnp.stack([ev_sgu_ln_g[i], ev_sgu_ln_b[i]])
            bias_full = jnp.repeat(ev_b_spatial[i].T, SGU_CH // SGU_GROUPS, axis=1)
            c = _even_front(x, w_in, i, ev_conv_w[i], cvec, svec, ev_w_spatial[i], bias_full)
        else:
            c = _attention(_qkv_proj(x, posc, freq, w_qkv, i))
        ln4 = jnp.stack([ln_g[layer, 0], ln_b[layer, 0], ln_g[layer, 1], ln_b[layer, 1]])
        x = _back(x.reshape(b * s, d), c.reshape(b * s, d), w_out[layer % 2], i,
                  w_gate, w_up, w_down, layer, ln4).reshape(b, s, d)
    return x
```

```python
import functools

import jax
import jax.numpy as jnp
from jax import lax
from jax.experimental import pallas as pl
from jax.experimental.pallas import tpu as pltpu

F32 = jnp.float32
BF16 = jnp.bfloat16

D_MODEL = 1024
DEPTH = 4
CONV_CH = 512
CONV_WIDTH = 31
SGU_CH = 512
SGU_GROUPS = 8
SGU_CHUNK = 128
HEADS = 8
HEAD_DIM = 128
ATTN_BLOCK = 128
ATTN_DILATIONS = (1, 4, 16)
ROT_DIM = 32
ROPE_THETA = 500000.0
FFN_HIDDEN = 2816
DN_ALPHA = (2.0 * DEPTH) ** 0.25
LN_EPS = 1e-5
NEG_BIG = -1e30

LANES = 128
SUBLANES = 8
CONV_ROWS = 64
VMEM_LIMIT_BYTES = 56 * 1024 * 1024

TM_BACK = 1024
BACK_ROW_CHAINS = (256, 256, 256, 256)
assert sum(BACK_ROW_CHAINS) == TM_BACK
TM_EVEN = 1024
EVEN_ROWS = 512
ATTN_TILE = 2048
ATTN_HEADS_PER_STEP = 2
N_SLABS = D_MODEL // LANES
R2_LANE = LANES // 2
QKV_ROW_CHUNKS = (512, 512, 512, 256, 256)
assert sum(QKV_ROW_CHUNKS) == ATTN_TILE
PERM_BLOCKS_PER_STEP = 6
CONV_HALO = 32


def _layer_norm(v, g, b):
    mu = jnp.mean(v, axis=-1, keepdims=True)
    c = v - mu
    var = jnp.mean(c * c, axis=-1, keepdims=True)
    return c * lax.rsqrt(var + LN_EPS) * g + b


def _resident(shape):
    nd = len(shape)
    return pl.BlockSpec(shape, lambda *_: (0,) * nd, pipeline_mode=pl.Buffered(1))


def _resident_layer(stack, layer):
    nd = stack.ndim - 1
    return pl.BlockSpec((None,) + stack.shape[1:], lambda *_: (layer,) + (0,) * nd,
                        pipeline_mode=pl.Buffered(1))


def _back_kernel(x_ref, c_ref, wo_ref, wg_ref, wu_ref, wd_ref, ln_ref, o_ref):
    rows = [slice(sum(BACK_ROW_CHAINS[:i]), sum(BACK_ROW_CHAINS[:i + 1])) for i in range(len(BACK_ROW_CHAINS))]
    hs = [jnp.dot(c_ref[r, :], wo_ref[...], preferred_element_type=F32) for r in rows]
    x1s = [_layer_norm(DN_ALPHA * x_ref[r, :] + h, ln_ref[0:1, :], ln_ref[1:2, :]) for r, h in zip(rows, hs)]
    acts = []
    for x1 in x1s:
        x1b = x1.astype(BF16)
        g = jnp.dot(x1b, wg_ref[...], preferred_element_type=F32)
        u = jnp.dot(x1b, wu_ref[...], preferred_element_type=F32)
        acts.append((g * jax.nn.sigmoid(g) * u).astype(BF16))
    for r, x1, a in zip(rows, x1s, acts):
        y = jnp.dot(a, wd_ref[...], preferred_element_type=F32)
        o_ref[r, :] = _layer_norm(DN_ALPHA * x1 + y, ln_ref[2:3, :], ln_ref[3:4, :])


def _back(x2d, c2d, w_out, mixer_idx, w_gate, w_up, w_down, layer, ln4):
    n = x2d.shape[0]
    tile = lambda: pl.BlockSpec((TM_BACK, D_MODEL), lambda i: (i, 0))
    return pl.pallas_call(
        _back_kernel,
        grid=(n // TM_BACK,),
        in_specs=[tile(), tile(),
                  _resident_layer(w_out, mixer_idx), _resident_layer(w_gate, layer),
                  _resident_layer(w_up, layer), _resident_layer(w_down, layer),
                  _resident(ln4.shape)],
        out_specs=tile(),
        out_shape=jax.ShapeDtypeStruct((n, D_MODEL), F32),
        compiler_params=pltpu.CompilerParams(
            dimension_semantics=("arbitrary",), vmem_limit_bytes=VMEM_LIMIT_BYTES),
        name="back",
    )(x2d, c2d, w_out, w_gate, w_up, w_down, ln4)


def _even_kernel(x_ref, win_ref, cw_ref, cvec_ref, svec_ref, wsp_ref, bsp_ref, o_ref, abuf_ref):
    tm, rg = TM_EVEN, EVEN_ROWS

    @pl.when(pl.program_id(1) == 0)
    def _():
        abuf_ref[0, 0:CONV_HALO, :] = jnp.zeros((CONV_HALO, CONV_CH), F32)

    row = lax.broadcasted_iota(jnp.int32, (SGU_CHUNK, SGU_CHUNK), 0)
    col = lax.broadcasted_iota(jnp.int32, (SGU_CHUNK, SGU_CHUNK), 1)
    tril = col <= row
    w_pairs = [jnp.concatenate([jnp.where(tril, wsp_ref[2 * p], 0.0), jnp.where(tril, wsp_ref[2 * p + 1], 0.0)],
                               axis=1).astype(BF16) for p in range(SGU_GROUPS // 2)]
    low_half = lax.broadcasted_iota(jnp.int32, (SGU_CHUNK, LANES), 1) < (LANES // 2)
    base = CONV_HALO - (CONV_WIDTH - 1)
    lag = CONV_HALO - SUBLANES

    for r0 in range(0, tm, rg):
        h = jnp.dot(x_ref[0, r0:r0 + rg, :].astype(BF16), win_ref[...], preferred_element_type=F32)

        abuf_ref[0, CONV_HALO + r0:CONV_HALO + r0 + rg, :] = (
            h[:, :CONV_CH] * jax.nn.sigmoid(h[:, CONV_CH:2 * CONV_CH]))
        lo = 0 if r0 == 0 else r0 + lag
        for r in range(1, SUBLANES):
            abuf_ref[r, lo:r0 + rg + lag, :] = abuf_ref[0, lo + r:r0 + rg + lag + r, :]
        for c0 in range(r0, r0 + rg, CONV_ROWS):
            conv = jnp.broadcast_to(cvec_ref[0:1, :], (CONV_ROWS, CONV_CH))
            for j in range(CONV_WIDTH):
                q, r = divmod(base + j, SUBLANES)
                conv = conv + cw_ref[j:j + 1, :] * abuf_ref[r, c0 + SUBLANES * q:c0 + SUBLANES * q + CONV_ROWS, :]
            an = _layer_norm(conv, cvec_ref[1:2, :], cvec_ref[2:3, :])
            o_ref[0, c0:c0 + CONV_ROWS, 0:CONV_CH] = (an * jax.nn.sigmoid(an)).astype(BF16)

        z = h[:, 2 * CONV_CH:]
        z = 0.5 * z * (1.0 + lax.erf(z * (2.0 ** -0.5)))
        u = z[:, :SGU_CH]
        v = _layer_norm(z[:, SGU_CH:], svec_ref[0:1, :], svec_ref[1:2, :])
        for c in range(rg // SGU_CHUNK):
            rows = slice(c * SGU_CHUNK, (c + 1) * SGU_CHUNK)
            for p in range(SGU_GROUPS // 2):
                v_slab = v[rows, p * LANES:(p + 1) * LANES]
                rhs = jnp.concatenate([jnp.where(low_half, v_slab, 0.0).astype(BF16),
                                       jnp.where(low_half, 0.0, v_slab).astype(BF16)], axis=0)
                sv = jnp.dot(w_pairs[p], rhs, preferred_element_type=F32) + bsp_ref[:, p * LANES:(p + 1) * LANES]
                o_ref[0, r0 + c * SGU_CHUNK:r0 + (c + 1) * SGU_CHUNK,
                      CONV_CH + p * LANES:CONV_CH + (p + 1) * LANES] = (
                    u[rows, p * LANES:(p + 1) * LANES] * sv).astype(BF16)

    abuf_ref[0, 0:CONV_HALO, :] = abuf_ref[0, tm:tm + CONV_HALO, :]


def _even_front(x, w_in, layer, conv_w, cvec, svec, w_spatial, bias_full):
    b, s, _ = x.shape
    return pl.pallas_call(
        _even_kernel,
        grid=(b, s // TM_EVEN),
        in_specs=[pl.BlockSpec((1, TM_EVEN, D_MODEL), lambda bi, i: (bi, i, 0)),
                  _resident_layer(w_in, layer), _resident(conv_w.shape), _resident(cvec.shape),
                  _resident(svec.shape), _resident(w_spatial.shape), _resident(bias_full.shape)],
        out_specs=pl.BlockSpec((1, TM_EVEN, D_MODEL), lambda bi, i: (bi, i, 0)),
        out_shape=jax.ShapeDtypeStruct((b, s, D_MODEL), BF16),
        scratch_shapes=[pltpu.VMEM((SUBLANES, CONV_HALO + TM_EVEN, CONV_CH), F32)],
        compiler_params=pltpu.CompilerParams(
            dimension_semantics=("arbitrary", "arbitrary"), vmem_limit_bytes=VMEM_LIMIT_BYTES),
        name="even_front",
    )(x, w_in, conv_w, cvec, svec, w_spatial, bias_full)


def _stream_blocks(dilation):
    if dilation == 1:
        return [(n * ATTN_BLOCK, 1) for n in range(ATTN_TILE // ATTN_BLOCK)]
    if dilation == 4:
        return [(n * ATTN_BLOCK * 4 + rho, 4) for n in range(4) for rho in range(4)]
    return [(rho, 16) for rho in range(16)]


def _qkv_kernel(*refs):
    x_slabs = refs[:N_SLABS]
    posc_ref, freq_ref, w_ref, o_ref = refs[N_SLABS:N_SLABS + 4]
    xp_refs = refs[N_SLABS + 4:N_SLABS + 7]
    cos_ref, sin_ref = refs[N_SLABS + 7:]
    c = pl.program_id(2)
    half = ROT_DIM // 2

    @pl.when(c == 0)
    def _():
        ang = posc_ref[0].astype(F32) * freq_ref[...]
        cos_c, sin_c = jnp.cos(ang), jnp.sin(ang)
        lane_c = lax.broadcasted_iota(jnp.int32, ang.shape, 1)

        def spread(tab, s):
            roll = lambda shift: tab if shift % LANES == 0 else pltpu.roll(tab, shift % LANES, axis=1)
            return roll(LANES - half * s), roll(LANES + R2_LANE - half * s)

        in_r1 = lane_c < half
        in_r2 = (lane_c >= R2_LANE) & (lane_c < R2_LANE + half)
        for s in range(SUBLANES):
            c1, c2 = spread(cos_c, s)
            s1, s2 = spread(sin_c, s)
            rows = pl.ds(s, ATTN_TILE // SUBLANES, stride=SUBLANES)
            cos_ref[0, rows, :] = jnp.where(in_r1, c1, jnp.where(in_r2, c2, 1.0))
            sin_ref[0, rows, :] = jnp.where(in_r1, -s1, jnp.where(in_r2, s2, 0.0))
        for sl in range(N_SLABS):
            xp_refs[0][:, sl * LANES:(sl + 1) * LANES] = x_slabs[sl][0].astype(BF16)
        for gi in (1, 2):
            for blk, (start, stride) in enumerate(_stream_blocks(ATTN_DILATIONS[gi])):
                rows = pl.ds(start, ATTN_BLOCK, stride=stride)
                dst = slice(blk * ATTN_BLOCK, (blk + 1) * ATTN_BLOCK)
                cos_ref[gi, dst, :] = cos_ref[0, rows, :]
                sin_ref[gi, dst, :] = sin_ref[0, rows, :]

    t = c % 3
    scale = jnp.where(t == 0, HEAD_DIM ** -0.5, 1.0).astype(F32)
    rotate = t < 2

    def project(gi):
        def permute_next(i):
            stride = ATTN_DILATIONS[gi + 1]
            blk = jnp.minimum(t * PERM_BLOCKS_PER_STEP + i, ATTN_TILE // ATTN_BLOCK - 1)
            if stride == 4:
                start = (blk >> 2) * (4 * ATTN_BLOCK) + (blk & 3)
            else:
                start = blk
            dst = pl.ds(pl.multiple_of(blk * ATTN_BLOCK, ATTN_BLOCK), ATTN_BLOCK)
            for sl in range(N_SLABS):
                xp_refs[gi + 1][dst, sl * LANES:(sl + 1) * LANES] = (
                    x_slabs[sl][0, pl.ds(start, ATTN_BLOCK, stride=stride), :].astype(BF16))

        n_rc = len(QKV_ROW_CHUNKS)
        for rc, n_rows in enumerate(QKV_ROW_CHUNKS):
            rows = slice(sum(QKV_ROW_CHUNKS[:rc]), sum(QKV_ROW_CHUNKS[:rc + 1]))
            if gi + 1 < len(ATTN_DILATIONS):
                for i in range(rc, PERM_BLOCKS_PER_STEP, n_rc):
                    permute_next(i)
            y = jnp.dot(xp_refs[gi][rows, :], w_ref[...], preferred_element_type=F32)
            cos = jnp.where(rotate, cos_ref[gi, rows, :], 1.0) * scale
            sin = jnp.where(rotate, sin_ref[gi, rows, :], 0.0) * scale
            for hd in range(HEADS):
                yh = y[:, hd * HEAD_DIM:(hd + 1) * HEAD_DIM]
                o_ref[0, 0, hd, rows, :] = (yh * cos + pltpu.roll(yh, R2_LANE, axis=1) * sin).astype(BF16)

    for gi in range(len(ATTN_DILATIONS)):
        pl.when(c // 3 == gi)(functools.partial(project, gi))


def _qkv_proj(x, posc, freq, w_qkv, layer):
    b, s, _ = x.shape
    n_chunks = w_qkv.shape[2] // D_MODEL
    slab = lambda sl: pl.BlockSpec((1, ATTN_TILE, LANES), lambda bi, ti, c: (bi, ti, sl))
    return pl.pallas_call(
        _qkv_kernel,
        grid=(b, s // ATTN_TILE, n_chunks),
        in_specs=[slab(sl) for sl in range(N_SLABS)] + [
                  pl.BlockSpec((1, ATTN_TILE // SUBLANES, LANES), lambda bi, ti, c: (bi, ti, 0)),
                  pl.BlockSpec((1, LANES), lambda bi, ti, c: (0, 0)),
                  pl.BlockSpec((None, D_MODEL, D_MODEL), lambda bi, ti, c: (layer, 0, c))],
        out_specs=pl.BlockSpec((1, 1, HEADS, ATTN_TILE, HEAD_DIM), lambda bi, ti, c: (c, bi, 0, ti, 0)),
        out_shape=jax.ShapeDtypeStruct((n_chunks, b, HEADS, s, HEAD_DIM), BF16),
        scratch_shapes=[pltpu.VMEM((ATTN_TILE, D_MODEL), BF16)] * 3 + [
                        pltpu.VMEM((3, ATTN_TILE, LANES), F32),
                        pltpu.VMEM((3, ATTN_TILE, LANES), F32)],
        compiler_params=pltpu.CompilerParams(
            dimension_semantics=("arbitrary", "arbitrary", "arbitrary"),
            vmem_limit_bytes=VMEM_LIMIT_BYTES),
        name="qkv_proj",
    )(*([x] * N_SLABS), posc, freq, w_qkv)


def _attn_kernel(q0, k0, v0, q1, k1, v1, q2, k2, v2,
                 pk0, pv0, pk1, pv1, pk2, pv2, o_ref, acc_ref, m_ref, l_ref):
    has_prev = pl.program_id(1) > 0
    qi = lax.broadcasted_iota(jnp.int32, (ATTN_BLOCK, 2 * ATTN_BLOCK), 0)
    kj = lax.broadcasted_iota(jnp.int32, (ATTN_BLOCK, 2 * ATTN_BLOCK), 1)
    band = (kj >= qi) & (kj <= qi + ATTN_BLOCK)
    band_first = band & ((kj >= ATTN_BLOCK) | has_prev)
    ones = jnp.ones((2 * ATTN_BLOCK, LANES), BF16)
    qs, ks, vs = (q0, q1, q2), (k0, k1, k2), (v0, v1, v2)
    pks, pvs = (pk0, pk1, pk2), (pv0, pv1, pv2)
    for hh in range(ATTN_HEADS_PER_STEP):
        blk_rows = lambda r, i, hh=hh: r[0, 0, hh, i * ATTN_BLOCK:(i + 1) * ATTN_BLOCK, :]
        g0 = hh * len(ATTN_DILATIONS)
        _attn_head(qs, ks, vs, pks, pvs, blk_rows, band, band_first, ones, acc_ref, m_ref, l_ref, g0)
        _merge_groups(acc_ref, m_ref, l_ref, g0, o_ref, hh)


def _merge_groups(acc_ref, m_ref, l_ref, g0, o_ref, hh):
    for c in range(ATTN_TILE // ATTN_BLOCK):
        rows = slice(c * ATTN_BLOCK, (c + 1) * ATTN_BLOCK)
        m0, m1, m2 = m_ref[g0, rows, :], m_ref[g0 + 1, rows, :], m_ref[g0 + 2, rows, :]
        mx = jnp.maximum(jnp.maximum(m0, m1), m2)
        e0, e1, e2 = jnp.exp(m0 - mx), jnp.exp(m1 - mx), jnp.exp(m2 - mx)
        num = e0 * acc_ref[g0, rows, :] + e1 * acc_ref[g0 + 1, rows, :] + e2 * acc_ref[g0 + 2, rows, :]
        den = e0 * l_ref[g0, rows, :] + e1 * l_ref[g0 + 1, rows, :] + e2 * l_ref[g0 + 2, rows, :]
        o_ref[0, rows, hh * HEAD_DIM:(hh + 1) * HEAD_DIM] = (num / den).astype(BF16)


def _attn_head(qs, ks, vs, pks, pvs, blk_rows, band, band_first, ones, acc_ref, m_ref, l_ref, g0):
    contract_last = (((1,), (1,)), ((), ()))
    for gi, dilation in enumerate(ATTN_DILATIONS):
        n_streams = {1: 1, 4: 4, 16: 16}[dilation]
        for blk, (start, stride) in enumerate(_stream_blocks(dilation)):
            if blk < n_streams:
                k_prev, v_prev = blk_rows(pks[gi], blk), blk_rows(pvs[gi], blk)
            else:
                k_prev, v_prev = blk_rows(ks[gi], blk - n_streams), blk_rows(vs[gi], blk - n_streams)
            kcat = jnp.concatenate([k_prev, blk_rows(ks[gi], blk)], axis=0)
            vext = jnp.concatenate(
                [jnp.concatenate([v_prev, blk_rows(vs[gi], blk)], axis=0), ones], axis=1)
            s = lax.dot_general(blk_rows(qs[gi], blk), kcat, contract_last,
                                preferred_element_type=F32)
            s = jnp.where(band_first if blk < n_streams else band, s, NEG_BIG)
            m = jnp.max(s, axis=-1, keepdims=True)
            p = jnp.exp(s - m)
            pvl = jnp.dot(p.astype(BF16), vext, preferred_element_type=F32)
            rows = pl.ds(start, ATTN_BLOCK, stride=stride) if stride > 1 else pl.ds(start, ATTN_BLOCK)
            acc_ref[g0 + gi, rows, :] = pvl[:, :HEAD_DIM]
            l_ref[g0 + gi, rows, :] = pvl[:, HEAD_DIM:]
            m_ref[g0 + gi, rows, :] = jnp.broadcast_to(m, (ATTN_BLOCK, LANES))


def _attention(qkv):
    _, b, _, s, _ = qkv.shape
    n_tiles = s // ATTN_TILE

    hps = ATTN_HEADS_PER_STEP

    def cur(chunk):
        return pl.BlockSpec((1, 1, hps, ATTN_TILE, HEAD_DIM), lambda bi, ti, hd: (chunk, bi, hd, ti, 0))

    def prev(chunk, rows):
        per_tile = ATTN_TILE // rows
        return pl.BlockSpec(
            (1, 1, hps, rows, HEAD_DIM),
            lambda bi, ti, hd: (chunk, bi, hd, jnp.maximum(ti * per_tile - 1, 0), 0))

    prev_rows = (ATTN_BLOCK, 4 * ATTN_BLOCK, ATTN_TILE)
    in_specs = [cur(c) for c in range(9)]
    operands = [qkv] * 9
    for gi in range(3):
        in_specs += [prev(3 * gi + 1, prev_rows[gi]), prev(3 * gi + 2, prev_rows[gi])]
        operands += [qkv, qkv]
    return pl.pallas_call(
        _attn_kernel,
        grid=(b, n_tiles, HEADS // hps),
        in_specs=in_specs,
        out_specs=pl.BlockSpec((1, ATTN_TILE, hps * HEAD_DIM), lambda bi, ti, hd: (bi, ti, hd)),
        out_shape=jax.ShapeDtypeStruct((b, s, HEADS * HEAD_DIM), BF16),
        scratch_shapes=[pltpu.VMEM((hps * len(ATTN_DILATIONS), ATTN_TILE, LANES), F32)] * 3,
        compiler_params=pltpu.CompilerParams(
            dimension_semantics=("arbitrary", "arbitrary", "arbitrary"),
            vmem_limit_bytes=VMEM_LIMIT_BYTES),
        name="attention",
    )(*operands)


def _qkv_weight_kernel(w_ref, o_ref):
    half = ROT_DIM // 2
    is_qk = pl.program_id(1) % 3 < 2
    lane = lax.broadcasted_iota(jnp.int32, (D_MODEL, HEAD_DIM), 1)
    keep = jnp.logical_not(is_qk) | (lane < half) | (lane >= R2_LANE + half)
    for hd in range(HEADS):
        w = w_ref[:, hd * HEAD_DIM:(hd + 1) * HEAD_DIM]
        moved = jnp.where(lane < R2_LANE, pltpu.roll(w, HEAD_DIM - half, axis=1),
                          pltpu.roll(w, R2_LANE - half, axis=1))
        o_ref[:, hd * HEAD_DIM:(hd + 1) * HEAD_DIM] = jnp.where(keep, w, moved).astype(BF16)


def _qkv_weights(w_qkv):
    n_layers, d, n_cols = w_qkv.shape
    block = lambda: pl.BlockSpec((None, d, D_MODEL), lambda li, c: (li, 0, c))
    return pl.pallas_call(
        _qkv_weight_kernel,
        grid=(n_layers, n_cols // D_MODEL),
        in_specs=[block()],
        out_specs=block(),
        out_shape=jax.ShapeDtypeStruct(w_qkv.shape, BF16),
        compiler_params=pltpu.CompilerParams(
            dimension_semantics=("arbitrary", "arbitrary"), vmem_limit_bytes=VMEM_LIMIT_BYTES),
        name="qkv_weights",
    )(w_qkv)


def kernel(x, positions, ev_w_in, ev_conv_w, ev_conv_b, ev_conv_ln_g, ev_conv_ln_b,
           ev_sgu_ln_g, ev_sgu_ln_b, ev_w_spatial, ev_b_spatial, ev_w_out,
           od_w_qkv, od_w_out, ffn_w_gate, ffn_w_up, ffn_w_down, ln_g, ln_b):
    b, s, d = x.shape
    assert (d, s % ATTN_TILE, (b * s) % TM_BACK) == (D_MODEL, 0, 0)
    posc = jnp.repeat(positions.reshape(b, s // SUBLANES, SUBLANES), LANES // SUBLANES, axis=2)
    inv_freq = ROPE_THETA ** (-jnp.arange(0, ROT_DIM, 2, dtype=F32) / ROT_DIM)
    freq = jnp.tile(inv_freq, LANES // (ROT_DIM // 2)).reshape(1, LANES).astype(F32)
    w_in, w_qkv = ev_w_in.astype(BF16), _qkv_weights(od_w_qkv)
    w_out = (ev_w_out.astype(BF16), od_w_out.astype(BF16))
    w_gate, w_up, w_down = ffn_w_gate.astype(BF16), ffn_w_up.astype(BF16), ffn_w_down.astype(BF16)

    for layer in range(DEPTH):
        i = layer // 2
        if layer % 2 == 0:
            cvec = jnp.stack([ev_conv_b[i], ev_conv_ln_g[i], ev_conv_ln_b[i]])
            svec = jnp.stack([ev_sgu_ln_g[i], ev_sgu_ln_b[i]])
            bias_full = jnp.repeat(ev_b_spatial[i].T, SGU_CH // SGU_GROUPS, axis=1)
            c = _even_front(x, w_in, i, ev_conv_w[i], cvec, svec, ev_w_spatial[i], bias_full)
        else:
            c = _attention(_qkv_proj(x, posc, freq, w_qkv, i))
        ln4 = jnp.stack([ln_g[layer, 0], ln_b[layer, 0], ln_g[layer, 1], ln_b[layer, 1]])
        x = _back(x.reshape(b * s, d), c.reshape(b * s, d), w_out[layer % 2], i,
                  w_gate, w_up, w_down, layer, ln4).reshape(b, s, d)
    return x
```

```python
import functools

import jax
import jax.numpy as jnp
from jax import lax
from jax.experimental import pallas as pl
from jax.experimental.pallas import tpu as pltpu

F32 = jnp.float32
BF16 = jnp.bfloat16

D_MODEL = 1024
DEPTH = 4
CONV_CH = 512
CONV_WIDTH = 31
SGU_CH = 512
SGU_GROUPS = 8
SGU_CHUNK = 128
HEADS = 8
HEAD_DIM = 128
ATTN_BLOCK = 128
ATTN_DILATIONS = (1, 4, 16)
ROT_DIM = 32
ROPE_THETA = 500000.0
FFN_HIDDEN = 2816
DN_ALPHA = (2.0 * DEPTH) ** 0.25
LN_EPS = 1e-5
NEG_BIG = -1e30

LANES = 128
SUBLANES = 8
CONV_ROWS = 64
VMEM_LIMIT_BYTES = 56 * 1024 * 1024

TM_BACK = 1024
BACK_ROW_CHAINS = (256, 256, 256, 256)
assert sum(BACK_ROW_CHAINS) == TM_BACK
TM_EVEN = 1024
EVEN_ROWS = 512
ATTN_TILE = 2048
ATTN_HEADS_PER_STEP = 2
N_SLABS = D_MODEL // LANES
R2_LANE = LANES // 2
QKV_ROW_CHUNKS = (512, 512, 512, 256, 256)
assert sum(QKV_ROW_CHUNKS) == ATTN_TILE
PERM_BLOCKS_PER_STEP = 6
CONV_HALO = 32


def _layer_norm(v, g, b):
    mu = jnp.mean(v, axis=-1, keepdims=True)
    c = v - mu
    var = jnp.mean(c * c, axis=-1, keepdims=True)
    return c * lax.rsqrt(var + LN_EPS) * g + b


def _resident(shape):
    nd = len(shape)
    return pl.BlockSpec(shape, lambda *_: (0,) * nd, pipeline_mode=pl.Buffered(1))


def _resident_layer(stack, layer):
    nd = stack.ndim - 1
    return pl.BlockSpec((None,) + stack.shape[1:], lambda *_: (layer,) + (0,) * nd,
                        pipeline_mode=pl.Buffered(1))


def _back_kernel(x_ref, c_ref, wo_ref, wg_ref, wu_ref, wd_ref, ln_ref, o_ref):
    rows = [slice(sum(BACK_ROW_CHAINS[:i]), sum(BACK_ROW_CHAINS[:i + 1])) for i in range(len(BACK_ROW_CHAINS))]
    hs = [jnp.dot(c_ref[r, :], wo_ref[...], preferred_element_type=F32) for r in rows]
    x1s = [_layer_norm(DN_ALPHA * x_ref[r, :] + h, ln_ref[0:1, :], ln_ref[1:2, :]) for r, h in zip(rows, hs)]
    acts = []
    for x1 in x1s:
        x1b = x1.astype(BF16)
        g = jnp.dot(x1b, wg_ref[...], preferred_element_type=F32)
        u = jnp.dot(x1b, wu_ref[...], preferred_element_type=F32)
        acts.append((g * jax.nn.sigmoid(g) * u).astype(BF16))
    for r, x1, a in zip(rows, x1s, acts):
        y = jnp.dot(a, wd_ref[...], preferred_element_type=F32)
        o_ref[r, :] = _layer_norm(DN_ALPHA * x1 + y, ln_ref[2:3, :], ln_ref[3:4, :])


def _back(x2d, c2d, w_out, mixer_idx, w_gate, w_up, w_down, layer, ln4):
    n = x2d.shape[0]
    tile = lambda: pl.BlockSpec((TM_BACK, D_MODEL), lambda i: (i, 0))
    return pl.pallas_call(
        _back_kernel,
        grid=(n // TM_BACK,),
        in_specs=[tile(), tile(),
                  _resident_layer(w_out, mixer_idx), _resident_layer(w_gate, layer),
                  _resident_layer(w_up, layer), _resident_layer(w_down, layer),
                  _resident(ln4.shape)],
        out_specs=tile(),
        out_shape=jax.ShapeDtypeStruct((n, D_MODEL), F32),
        compiler_params=pltpu.CompilerParams(
            dimension_semantics=("arbitrary",), vmem_limit_bytes=VMEM_LIMIT_BYTES),
        name="back",
    )(x2d, c2d, w_out, w_gate, w_up, w_down, ln4)


def _even_kernel(x_ref, win_ref, cw_ref, cvec_ref, svec_ref, wsp_ref, bsp_ref, o_ref, abuf_ref):
    tm, rg = TM_EVEN, EVEN_ROWS

    @pl.when(pl.program_id(1) == 0)
    def _():
        abuf_ref[0, 0:CONV_HALO, :] = jnp.zeros((CONV_HALO, CONV_CH), F32)

    row = lax.broadcasted_iota(jnp.int32, (SGU_CHUNK, SGU_CHUNK), 0)
    col = lax.broadcasted_iota(jnp.int32, (SGU_CHUNK, SGU_CHUNK), 1)
    tril = col <= row
    w_pairs = [jnp.concatenate([jnp.where(tril, wsp_ref[2 * p], 0.0), jnp.where(tril, wsp_ref[2 * p + 1], 0.0)],
                               axis=1).astype(BF16) for p in range(SGU_GROUPS // 2)]
    low_half = lax.broadcasted_iota(jnp.int32, (SGU_CHUNK, LANES), 1) < (LANES // 2)
    base = CONV_HALO - (CONV_WIDTH - 1)
    lag = CONV_HALO - SUBLANES

    for r0 in range(0, tm, rg):
        h = jnp.dot(x_ref[0, r0:r0 + rg, :].astype(BF16), win_ref[...], preferred_element_type=F32)

        abuf_ref[0, CONV_HALO + r0:CONV_HALO + r0 + rg, :] = (
            h[:, :CONV_CH] * jax.nn.sigmoid(h[:, CONV_CH:2 * CONV_CH]))
        lo = 0 if r0 == 0 else r0 + lag
        for r in range(1, SUBLANES):
            abuf_ref[r, lo:r0 + rg + lag, :] = abuf_ref[0, lo + r:r0 + rg + lag + r, :]
        for c0 in range(r0, r0 + rg, CONV_ROWS):
            conv = jnp.broadcast_to(cvec_ref[0:1, :], (CONV_ROWS, CONV_CH))
            for j in range(CONV_WIDTH):
                q, r = divmod(base + j, SUBLANES)
                conv = conv + cw_ref[j:j + 1, :] * abuf_ref[r, c0 + SUBLANES * q:c0 + SUBLANES * q + CONV_ROWS, :]
            an = _layer_norm(conv, cvec_ref[1:2, :], cvec_ref[2:3, :])
            o_ref[0, c0:c0 + CONV_ROWS, 0:CONV_CH] = (an * jax.nn.sigmoid(an)).astype(BF16)

        z = h[:, 2 * CONV_CH:]
        z = 0.5 * z * (1.0 + lax.erf(z * (2.0 ** -0.5)))
        u = z[:, :SGU_CH]
        v = _layer_norm(z[:, SGU_CH:], svec_ref[0:1, :], svec_ref[1:2, :])
        for c in range(rg // SGU_CHUNK):
            rows = slice(c * SGU_CHUNK, (c + 1) * SGU_CHUNK)
            for p in range(SGU_GROUPS // 2):
                v_slab = v[rows, p * LANES:(p + 1) * LANES]
                rhs = jnp.concatenate([jnp.where(low_half, v_slab, 0.0).astype(BF16),
                                       jnp.where(low_half, 0.0, v_slab).astype(BF16)], axis=0)
                sv = jnp.dot(w_pairs[p], rhs, preferred_element_type=F32) + bsp_ref[:, p * LANES:(p + 1) * LANES]
                o_ref[0, r0 + c * SGU_CHUNK:r0 + (c + 1) * SGU_CHUNK,
                      CONV_CH + p * LANES:CONV_CH + (p + 1) * LANES] = (
                    u[rows, p * LANES:(p + 1) * LANES] * sv).astype(BF16)

    abuf_ref[0, 0:CONV_HALO, :] = abuf_ref[0, tm:tm + CONV_HALO, :]


def _even_front(x, w_in, layer, conv_w, cvec, svec, w_spatial, bias_full):
    b, s, _ = x.shape
    return pl.pallas_call(
        _even_kernel,
        grid=(b, s // TM_EVEN),
        in_specs=[pl.BlockSpec((1, TM_EVEN, D_MODEL), lambda bi, i: (bi, i, 0)),
                  _resident_layer(w_in, layer), _resident(conv_w.shape), _resident(cvec.shape),
                  _resident(svec.shape), _resident(w_spatial.shape), _resident(bias_full.shape)],
        out_specs=pl.BlockSpec((1, TM_EVEN, D_MODEL), lambda bi, i: (bi, i, 0)),
        out_shape=jax.ShapeDtypeStruct((b, s, D_MODEL), BF16),
        scratch_shapes=[pltpu.VMEM((SUBLANES, CONV_HALO + TM_EVEN, CONV_CH), F32)],
        compiler_params=pltpu.CompilerParams(
            dimension_semantics=("arbitrary", "arbitrary"), vmem_limit_bytes=VMEM_LIMIT_BYTES),
        name="even_front",
    )(x, w_in, conv_w, cvec, svec, w_spatial, bias_full)


def _stream_blocks(dilation):
    if dilation == 1:
        return [(n * ATTN_BLOCK, 1) for n in range(ATTN_TILE // ATTN_BLOCK)]
    if dilation == 4:
        return [(n * ATTN_BLOCK * 4 + rho, 4) for n in range(4) for rho in range(4)]
    return [(rho, 16) for rho in range(16)]


def _qkv_kernel(*refs):
    x_slabs = refs[:N_SLABS]
    posc_ref, freq_ref, w_ref, o_ref = refs[N_SLABS:N_SLABS + 4]
    xp_refs = refs[N_SLABS + 4:N_SLABS + 7]
    cos_ref, sin_ref = refs[N_SLABS + 7:]
    c = pl.program_id(2)
    half = ROT_DIM // 2

    @pl.when(c == 0)
    def _():
        ang = posc_ref[0].astype(F32) * freq_ref[...]
        cos_c, sin_c = jnp.cos(ang), jnp.sin(ang)
        lane_c = lax.broadcasted_iota(jnp.int32, ang.shape, 1)

        def spread(tab, s):
            roll = lambda shift: tab if shift % LANES == 0 else pltpu.roll(tab, shift % LANES, axis=1)
            return roll(LANES - half * s), roll(LANES + R2_LANE - half * s)

        in_r1 = lane_c < half
        in_r2 = (lane_c >= R2_LANE) & (lane_c < R2_LANE + half)
        for s in range(SUBLANES):
            c1, c2 = spread(cos_c, s)
            s1, s2 = spread(sin_c, s)
            rows = pl.ds(s, ATTN_TILE // SUBLANES, stride=SUBLANES)
            cos_ref[0, rows, :] = jnp.where(in_r1, c1, jnp.where(in_r2, c2, 1.0))
            sin_ref[0, rows, :] = jnp.where(in_r1, -s1, jnp.where(in_r2, s2, 0.0))
        for sl in range(N_SLABS):
            xp_refs[0][:, sl * LANES:(sl + 1) * LANES] = x_slabs[sl][0].astype(BF16)
        for gi in (1, 2):
            for blk, (start, stride) in enumerate(_stream_blocks(ATTN_DILATIONS[gi])):
                rows = pl.ds(start, ATTN_BLOCK, stride=stride)
                dst = slice(blk * ATTN_BLOCK, (blk + 1) * ATTN_BLOCK)
                cos_ref[gi, dst, :] = cos_ref[0, rows, :]
                sin_ref[gi, dst, :] = sin_ref[0, rows, :]

    t = c % 3
    scale = jnp.where(t == 0, HEAD_DIM ** -0.5, 1.0).astype(F32)
    rotate = t < 2

    def project(gi):
        def permute_next(i):
            stride = ATTN_DILATIONS[gi + 1]
            blk = jnp.minimum(t * PERM_BLOCKS_PER_STEP + i, ATTN_TILE // ATTN_BLOCK - 1)
            if stride == 4:
                start = (blk >> 2) * (4 * ATTN_BLOCK) + (blk & 3)
            else:
                start = blk
            dst = pl.ds(pl.multiple_of(blk * ATTN_BLOCK, ATTN_BLOCK), ATTN_BLOCK)
            for sl in range(N_SLABS):
                xp_refs[gi + 1][dst, sl * LANES:(sl + 1) * LANES] = (
                    x_slabs[sl][0, pl.ds(start, ATTN_BLOCK, stride=stride), :].astype(BF16))

        n_rc = len(QKV_ROW_CHUNKS)
        for rc, n_rows in enumerate(QKV_ROW_CHUNKS):
            rows = slice(sum(QKV_ROW_CHUNKS[:rc]), sum(QKV_ROW_CHUNKS[:rc + 1]))
            if gi + 1 < len(ATTN_DILATIONS):
                for i in range(rc, PERM_BLOCKS_PER_STEP, n_rc):
                    permute_next(i)
            y = jnp.dot(xp_refs[gi][rows, :], w_ref[...], preferred_element_type=F32)
            cos = jnp.where(rotate, cos_ref[gi, rows, :], 1.0) * scale
            sin = jnp.where(rotate, sin_ref[gi, rows, :], 0.0) * scale
            for hd in range(HEADS):
                yh = y[:, hd * HEAD_DIM:(hd + 1) * HEAD_DIM]
                o_ref[0, 0, hd, rows, :] = (yh * cos + pltpu.roll(yh, R2_LANE, axis=1) * sin).astype(BF16)

    for gi in range(len(ATTN_DILATIONS)):
        pl.when(c // 3 == gi)(functools.partial(project, gi))


def _qkv_proj(x, posc, freq, w_qkv, layer):
    b, s, _ = x.shape
    n_chunks = w_qkv.shape[1]
    slab = lambda sl: pl.BlockSpec((1, ATTN_TILE, LANES), lambda bi, ti, c: (bi, ti, sl))
    return pl.pallas_call(
        _qkv_kernel,
        grid=(b, s // ATTN_TILE, n_chunks),
        in_specs=[slab(sl) for sl in range(N_SLABS)] + [
                  pl.BlockSpec((1, ATTN_TILE // SUBLANES, LANES), lambda bi, ti, c: (bi, ti, 0)),
                  pl.BlockSpec((1, LANES), lambda bi, ti, c: (0, 0)),
                  pl.BlockSpec((None, None, D_MODEL, D_MODEL), lambda bi, ti, c: (layer, c, 0, 0))],
        out_specs=pl.BlockSpec((1, 1, HEADS, ATTN_TILE, HEAD_DIM), lambda bi, ti, c: (c, bi, 0, ti, 0)),
        out_shape=jax.ShapeDtypeStruct((n_chunks, b, HEADS, s, HEAD_DIM), BF16),
        scratch_shapes=[pltpu.VMEM((ATTN_TILE, D_MODEL), BF16)] * 3 + [
                        pltpu.VMEM((3, ATTN_TILE, LANES), F32),
                        pltpu.VMEM((3, ATTN_TILE, LANES), F32)],
        compiler_params=pltpu.CompilerParams(
            dimension_semantics=("arbitrary", "arbitrary", "arbitrary"),
            vmem_limit_bytes=VMEM_LIMIT_BYTES),
        name="qkv_proj",
    )(*([x] * N_SLABS), posc, freq, w_qkv)


def _attn_kernel(q0, k0, v0, q1, k1, v1, q2, k2, v2,
                 pk0, pv0, pk1, pv1, pk2, pv2, o_ref, acc_ref, m_ref, l_ref):
    has_prev = pl.program_id(1) > 0
    qi = lax.broadcasted_iota(jnp.int32, (ATTN_BLOCK, 2 * ATTN_BLOCK), 0)
    kj = lax.broadcasted_iota(jnp.int32, (ATTN_BLOCK, 2 * ATTN_BLOCK), 1)
    band = (kj >= qi) & (kj <= qi + ATTN_BLOCK)
    band_first = band & ((kj >= ATTN_BLOCK) | has_prev)
    ones = jnp.ones((2 * ATTN_BLOCK, LANES), BF16)
    qs, ks, vs = (q0, q1, q2), (k0, k1, k2), (v0, v1, v2)
    pks, pvs = (pk0, pk1, pk2), (pv0, pv1, pv2)
    for hh in range(ATTN_HEADS_PER_STEP):
        blk_rows = lambda r, i, hh=hh: r[0, 0, hh, i * ATTN_BLOCK:(i + 1) * ATTN_BLOCK, :]
        g0 = hh * len(ATTN_DILATIONS)
        _attn_head(qs, ks, vs, pks, pvs, blk_rows, band, band_first, ones, acc_ref, m_ref, l_ref, g0)
        _merge_groups(acc_ref, m_ref, l_ref, g0, o_ref, hh)


def _merge_groups(acc_ref, m_ref, l_ref, g0, o_ref, hh):
    for c in range(ATTN_TILE // ATTN_BLOCK):
        rows = slice(c * ATTN_BLOCK, (c + 1) * ATTN_BLOCK)
        m0, m1, m2 = m_ref[g0, rows, :], m_ref[g0 + 1, rows, :], m_ref[g0 + 2, rows, :]
        mx = jnp.maximum(jnp.maximum(m0, m1), m2)
        e0, e1, e2 = jnp.exp(m0 - mx), jnp.exp(m1 - mx), jnp.exp(m2 - mx)
        num = e0 * acc_ref[g0, rows, :] + e1 * acc_ref[g0 + 1, rows, :] + e2 * acc_ref[g0 + 2, rows, :]
        den = e0 * l_ref[g0, rows, :] + e1 * l_ref[g0 + 1, rows, :] + e2 * l_ref[g0 + 2, rows, :]
        o_ref[0, rows, hh * HEAD_DIM:(hh + 1) * HEAD_DIM] = (num / den).astype(BF16)


def _attn_head(qs, ks, vs, pks, pvs, blk_rows, band, band_first, ones, acc_ref, m_ref, l_ref, g0):
    contract_last = (((1,), (1,)), ((), ()))
    for gi, dilation in enumerate(ATTN_DILATIONS):
        n_streams = {1: 1, 4: 4, 16: 16}[dilation]
        for blk, (start, stride) in enumerate(_stream_blocks(dilation)):
            if blk < n_streams:
                k_prev, v_prev = blk_rows(pks[gi], blk), blk_rows(pvs[gi], blk)
            else:
                k_prev, v_prev = blk_rows(ks[gi], blk - n_streams), blk_rows(vs[gi], blk - n_streams)
            kcat = jnp.concatenate([k_prev, blk_rows(ks[gi], blk)], axis=0)
            vext = jnp.concatenate(
                [jnp.concatenate([v_prev, blk_rows(vs[gi], blk)], axis=0), ones], axis=1)
            s = lax.dot_general(blk_rows(qs[gi], blk), kcat, contract_last,
                                preferred_element_type=F32)
            s = jnp.where(band_first if blk < n_streams else band, s, NEG_BIG)
            m = jnp.max(s, axis=-1, keepdims=True)
            p = jnp.exp(s - m)
            pvl = jnp.dot(p.astype(BF16), vext, preferred_element_type=F32)
            rows = pl.ds(start, ATTN_BLOCK, stride=stride) if stride > 1 else pl.ds(start, ATTN_BLOCK)
            acc_ref[g0 + gi, rows, :] = pvl[:, :HEAD_DIM]
            l_ref[g0 + gi, rows, :] = pvl[:, HEAD_DIM:]
            m_ref[g0 + gi, rows, :] = jnp.broadcast_to(m, (ATTN_BLOCK, LANES))


def _attention(qkv):
    _, b, _, s, _ = qkv.shape
    n_tiles = s // ATTN_TILE

    hps = ATTN_HEADS_PER_STEP

    def cur(chunk):
        return pl.BlockSpec((1, 1, hps, ATTN_TILE, HEAD_DIM), lambda bi, ti, hd: (chunk, bi, hd, ti, 0))

    def prev(chunk, rows):
        per_tile = ATTN_TILE // rows
        return pl.BlockSpec(
            (1, 1, hps, rows, HEAD_DIM),
            lambda bi, ti, hd: (chunk, bi, hd, jnp.maximum(ti * per_tile - 1, 0), 0))

    prev_rows = (ATTN_BLOCK, 4 * ATTN_BLOCK, ATTN_TILE)
    in_specs = [cur(c) for c in range(9)]
    operands = [qkv] * 9
    for gi in range(3):
        in_specs += [prev(3 * gi + 1, prev_rows[gi]), prev(3 * gi + 2, prev_rows[gi])]
        operands += [qkv, qkv]
    return pl.pallas_call(
        _attn_kernel,
        grid=(b, n_tiles, HEADS // hps),
        in_specs=in_specs,
        out_specs=pl.BlockSpec((1, ATTN_TILE, hps * HEAD_DIM), lambda bi, ti, hd: (bi, ti, hd)),
        out_shape=jax.ShapeDtypeStruct((b, s, HEADS * HEAD_DIM), BF16),
        scratch_shapes=[pltpu.VMEM((hps * len(ATTN_DILATIONS), ATTN_TILE, LANES), F32)] * 3,
        compiler_params=pltpu.CompilerParams(
            dimension_semantics=("arbitrary", "arbitrary", "arbitrary"),
            vmem_limit_bytes=VMEM_LIMIT_BYTES),
        name="attention",
    )(*operands)


def _qkv_weight_kernel(w_ref, o_ref):
    half = ROT_DIM // 2
    is_qk = pl.program_id(1) % 3 < 2
    lane = lax.broadcasted_iota(jnp.int32, (D_MODEL, HEAD_DIM), 1)
    keep = jnp.logical_not(is_qk) | (lane < half) | (lane >= R2_LANE + half)
    for hd in range(HEADS):
        w = w_ref[:, hd * HEAD_DIM:(hd + 1) * HEAD_DIM]
        moved = jnp.where(lane < R2_LANE, pltpu.roll(w, HEAD_DIM - half, axis=1),
                          pltpu.roll(w, R2_LANE - half, axis=1))
        o_ref[:, hd * HEAD_DIM:(hd + 1) * HEAD_DIM] = jnp.where(keep, w, moved).astype(BF16)


def _qkv_weights(w_qkv):
    n_layers, d, n_cols = w_qkv.shape
    n_chunks = n_cols // D_MODEL
    return pl.pallas_call(
        _qkv_weight_kernel,
        grid=(n_layers, n_chunks),
        in_specs=[pl.BlockSpec((None, d, D_MODEL), lambda li, c: (li, 0, c))],
        out_specs=pl.BlockSpec((None, None, d, D_MODEL), lambda li, c: (li, c, 0, 0)),
        out_shape=jax.ShapeDtypeStruct((n_layers, n_chunks, d, D_MODEL), BF16),
        compiler_params=pltpu.CompilerParams(
            dimension_semantics=("arbitrary", "arbitrary"), vmem_limit_bytes=VMEM_LIMIT_BYTES),
        name="qkv_weights",
    )(w_qkv)


def kernel(x, positions, ev_w_in, ev_conv_w, ev_conv_b, ev_conv_ln_g, ev_conv_ln_b,
           ev_sgu_ln_g, ev_sgu_ln_b, ev_w_spatial, ev_b_spatial, ev_w_out,
           od_w_qkv, od_w_out, ffn_w_gate, ffn_w_up, ffn_w_down, ln_g, ln_b):
    b, s, d = x.shape
    assert (d, s % ATTN_TILE, (b * s) % TM_BACK) == (D_MODEL, 0, 0)
    posc = jnp.repeat(positions.reshape(b, s // SUBLANES, SUBLANES), LANES // SUBLANES, axis=2)
    inv_freq = ROPE_THETA ** (-jnp.arange(0, ROT_DIM, 2, dtype=F32) / ROT_DIM)
    freq = jnp.tile(inv_freq, LANES // (ROT_DIM // 2)).reshape(1, LANES).astype(F32)
    w_in, w_qkv = ev_w_in.astype(BF16), _qkv_weights(od_w_qkv)
    w_out = (ev_w_out.astype(BF16), od_w_out.astype(BF16))
    w_gate, w_up, w_down = ffn_w_gate.astype(BF16), ffn_w_up.astype(BF16), ffn_w_down.astype(BF16)

    for layer in range(DEPTH):
        i = layer // 2
        if layer % 2 == 0:
            cvec = jnp.stack([ev_conv_b[i], ev_conv_ln_g[i], ev_conv_ln_b[i]])
            svec = jnp.stack([ev_sgu_ln_g[i], ev_sgu_ln_b[i]])
            bias_full = jnp.repeat(ev_b_spatial[i].T, SGU_CH // SGU_GROUPS, axis=1)
            c = _even_front(x, w_in, i, ev_conv_w[i], cvec, svec, ev_w_spatial[i], bias_full)
        else:
            c = _attention(_qkv_proj(x, posc, freq, w_qkv, i))
        ln4 = jnp.stack([ln_g[layer, 0], ln_b[layer, 0], ln_g[layer, 1], ln_b[layer, 1]])
        x = _back(x.reshape(b * s, d), c.reshape(b * s, d), w_out[layer % 2], i,
                  w_gate, w_up, w_down, layer, ln4).reshape(b, s, d)
    return x
```

```python
import functools

import jax
import jax.numpy as jnp
from jax import lax
from jax.experimental import pallas as pl
from jax.experimental.pallas import tpu as pltpu

F32 = jnp.float32
BF16 = jnp.bfloat16

D_MODEL = 1024
DEPTH = 4
CONV_CH = 512
CONV_WIDTH = 31
SGU_CH = 512
SGU_GROUPS = 8
SGU_CHUNK = 128
HEADS = 8
HEAD_DIM = 128
ATTN_BLOCK = 128
ATTN_DILATIONS = (1, 4, 16)
ROT_DIM = 32
ROPE_THETA = 500000.0
FFN_HIDDEN = 2816
DN_ALPHA = (2.0 * DEPTH) ** 0.25
LN_EPS = 1e-5
NEG_BIG = -1e30

LANES = 128
SUBLANES = 8
CONV_ROWS = 64
VMEM_LIMIT_BYTES = 56 * 1024 * 1024

TM_BACK = 1024
BACK_ROW_CHAINS = (256, 256, 256, 256)
assert sum(BACK_ROW_CHAINS) == TM_BACK
TM_EVEN = 1024
EVEN_ROWS = 512
ATTN_TILE = 2048
ATTN_HEADS_PER_STEP = 2
N_SLABS = D_MODEL // LANES
R2_LANE = LANES // 2
QKV_ROW_CHUNKS = (512, 512, 512, 256, 256)
assert sum(QKV_ROW_CHUNKS) == ATTN_TILE
PERM_BLOCKS_PER_STEP = 6
CONV_HALO = 32


def _layer_norm(v, g, b):
    mu = jnp.mean(v, axis=-1, keepdims=True)
    c = v - mu
    var = jnp.mean(c * c, axis=-1, keepdims=True)
    return c * lax.rsqrt(var + LN_EPS) * g + b


def _resident(shape):
    nd = len(shape)
    return pl.BlockSpec(shape, lambda *_: (0,) * nd)


def _resident_layer(stack, layer, single_buffer=False):
    nd = stack.ndim - 1
    mode = dict(pipeline_mode=pl.Buffered(1)) if single_buffer else {}
    return pl.BlockSpec((None,) + stack.shape[1:], lambda *_: (layer,) + (0,) * nd, **mode)


def _back_kernel(x_ref, c_ref, wo_ref, wg_ref, wu_ref, wd_ref, ln_ref, o_ref):
    rows = [slice(sum(BACK_ROW_CHAINS[:i]), sum(BACK_ROW_CHAINS[:i + 1])) for i in range(len(BACK_ROW_CHAINS))]
    hs = [jnp.dot(c_ref[r, :], wo_ref[...], preferred_element_type=F32) for r in rows]
    x1s = [_layer_norm(DN_ALPHA * x_ref[r, :] + h, ln_ref[0:1, :], ln_ref[1:2, :]) for r, h in zip(rows, hs)]
    acts = []
    for x1 in x1s:
        x1b = x1.astype(BF16)
        g = jnp.dot(x1b, wg_ref[...], preferred_element_type=F32)
        u = jnp.dot(x1b, wu_ref[...], preferred_element_type=F32)
        acts.append((g * jax.nn.sigmoid(g) * u).astype(BF16))
    for r, x1, a in zip(rows, x1s, acts):
        y = jnp.dot(a, wd_ref[...], preferred_element_type=F32)
        o_ref[r, :] = _layer_norm(DN_ALPHA * x1 + y, ln_ref[2:3, :], ln_ref[3:4, :])


def _back(x2d, c2d, w_out, mixer_idx, w_gate, w_up, w_down, layer, ln4):
    n = x2d.shape[0]
    tile = lambda: pl.BlockSpec((TM_BACK, D_MODEL), lambda i: (i, 0))
    return pl.pallas_call(
        _back_kernel,
        grid=(n // TM_BACK,),
        in_specs=[tile(), tile(),
                  _resident_layer(w_out, mixer_idx), _resident_layer(w_gate, layer, single_buffer=True),
                  _resident_layer(w_up, layer, single_buffer=True),
                  _resident_layer(w_down, layer, single_buffer=True),
                  _resident(ln4.shape)],
        out_specs=tile(),
        out_shape=jax.ShapeDtypeStruct((n, D_MODEL), F32),
        compiler_params=pltpu.CompilerParams(
            dimension_semantics=("arbitrary",), vmem_limit_bytes=VMEM_LIMIT_BYTES),
        name="back",
    )(x2d, c2d, w_out, w_gate, w_up, w_down, ln4)


def _even_kernel(x_ref, win_ref, cw_ref, cvec_ref, svec_ref, wsp_ref, bsp_ref, o_ref, abuf_ref):
    tm, rg = TM_EVEN, EVEN_ROWS

    @pl.when(pl.program_id(1) == 0)
    def _():
        abuf_ref[0, 0:CONV_HALO, :] = jnp.zeros((CONV_HALO, CONV_CH), F32)

    row = lax.broadcasted_iota(jnp.int32, (SGU_CHUNK, SGU_CHUNK), 0)
    col = lax.broadcasted_iota(jnp.int32, (SGU_CHUNK, SGU_CHUNK), 1)
    tril = col <= row
    w_pairs = [jnp.concatenate([jnp.where(tril, wsp_ref[2 * p], 0.0), jnp.where(tril, wsp_ref[2 * p + 1], 0.0)],
                               axis=1).astype(BF16) for p in range(SGU_GROUPS // 2)]
    low_half = lax.broadcasted_iota(jnp.int32, (SGU_CHUNK, LANES), 1) < (LANES // 2)
    base = CONV_HALO - (CONV_WIDTH - 1)
    lag = CONV_HALO - SUBLANES

    for r0 in range(0, tm, rg):
        h = jnp.dot(x_ref[0, r0:r0 + rg, :].astype(BF16), win_ref[...], preferred_element_type=F32)

        abuf_ref[0, CONV_HALO + r0:CONV_HALO + r0 + rg, :] = (
            h[:, :CONV_CH] * jax.nn.sigmoid(h[:, CONV_CH:2 * CONV_CH]))
        lo = 0 if r0 == 0 else r0 + lag
        for r in range(1, SUBLANES):
            abuf_ref[r, lo:r0 + rg + lag, :] = abuf_ref[0, lo + r:r0 + rg + lag + r, :]
        for c0 in range(r0, r0 + rg, CONV_ROWS):
            conv = jnp.broadcast_to(cvec_ref[0:1, :], (CONV_ROWS, CONV_CH))
            for j in range(CONV_WIDTH):
                q, r = divmod(base + j, SUBLANES)
                conv = conv + cw_ref[j:j + 1, :] * abuf_ref[r, c0 + SUBLANES * q:c0 + SUBLANES * q + CONV_ROWS, :]
            an = _layer_norm(conv, cvec_ref[1:2, :], cvec_ref[2:3, :])
            o_ref[0, c0:c0 + CONV_ROWS, 0:CONV_CH] = (an * jax.nn.sigmoid(an)).astype(BF16)

        z = h[:, 2 * CONV_CH:]
        z = 0.5 * z * (1.0 + lax.erf(z * (2.0 ** -0.5)))
        u = z[:, :SGU_CH]
        v = _layer_norm(z[:, SGU_CH:], svec_ref[0:1, :], svec_ref[1:2, :])
        for c in range(rg // SGU_CHUNK):
            rows = slice(c * SGU_CHUNK, (c + 1) * SGU_CHUNK)
            for p in range(SGU_GROUPS // 2):
                v_slab = v[rows, p * LANES:(p + 1) * LANES]
                rhs = jnp.concatenate([jnp.where(low_half, v_slab, 0.0).astype(BF16),
                                       jnp.where(low_half, 0.0, v_slab).astype(BF16)], axis=0)
                sv = jnp.dot(w_pairs[p], rhs, preferred_element_type=F32) + bsp_ref[:, p * LANES:(p + 1) * LANES]
                o_ref[0, r0 + c * SGU_CHUNK:r0 + (c + 1) * SGU_CHUNK,
                      CONV_CH + p * LANES:CONV_CH + (p + 1) * LANES] = (
                    u[rows, p * LANES:(p + 1) * LANES] * sv).astype(BF16)

    abuf_ref[0, 0:CONV_HALO, :] = abuf_ref[0, tm:tm + CONV_HALO, :]


def _even_front(x, w_in, layer, conv_w, cvec, svec, w_spatial, bias_full):
    b, s, _ = x.shape
    return pl.pallas_call(
        _even_kernel,
        grid=(b, s // TM_EVEN),
        in_specs=[pl.BlockSpec((1, TM_EVEN, D_MODEL), lambda bi, i: (bi, i, 0)),
                  _resident_layer(w_in, layer), _resident(conv_w.shape), _resident(cvec.shape),
                  _resident(svec.shape), _resident(w_spatial.shape), _resident(bias_full.shape)],
        out_specs=pl.BlockSpec((1, TM_EVEN, D_MODEL), lambda bi, i: (bi, i, 0)),
        out_shape=jax.ShapeDtypeStruct((b, s, D_MODEL), BF16),
        scratch_shapes=[pltpu.VMEM((SUBLANES, CONV_HALO + TM_EVEN, CONV_CH), F32)],
        compiler_params=pltpu.CompilerParams(
            dimension_semantics=("arbitrary", "arbitrary"), vmem_limit_bytes=VMEM_LIMIT_BYTES),
        name="even_front",
    )(x, w_in, conv_w, cvec, svec, w_spatial, bias_full)


def _stream_blocks(dilation):
    if dilation == 1:
        return [(n * ATTN_BLOCK, 1) for n in range(ATTN_TILE // ATTN_BLOCK)]
    if dilation == 4:
        return [(n * ATTN_BLOCK * 4 + rho, 4) for n in range(4) for rho in range(4)]
    return [(rho, 16) for rho in range(16)]


def _qkv_kernel(*refs):
    x_slabs = refs[:N_SLABS]
    posc_ref, freq_ref, w_ref, o_ref = refs[N_SLABS:N_SLABS + 4]
    xp_refs = refs[N_SLABS + 4:N_SLABS + 7]
    cos_ref, sin_ref = refs[N_SLABS + 7:]
    c = pl.program_id(2)
    half = ROT_DIM // 2

    @pl.when(c == 0)
    def _():
        ang = posc_ref[0].astype(F32) * freq_ref[...]
        cos_c, sin_c = jnp.cos(ang), jnp.sin(ang)
        lane_c = lax.broadcasted_iota(jnp.int32, ang.shape, 1)

        def spread(tab, s):
            roll = lambda shift: tab if shift % LANES == 0 else pltpu.roll(tab, shift % LANES, axis=1)
            return roll(LANES - half * s), roll(LANES + R2_LANE - half * s)

        in_r1 = lane_c < half
        in_r2 = (lane_c >= R2_LANE) & (lane_c < R2_LANE + half)
        for s in range(SUBLANES):
            c1, c2 = spread(cos_c, s)
            s1, s2 = spread(sin_c, s)
            rows = pl.ds(s, ATTN_TILE // SUBLANES, stride=SUBLANES)
            cos_ref[0, rows, :] = jnp.where(in_r1, c1, jnp.where(in_r2, c2, 1.0))
            sin_ref[0, rows, :] = jnp.where(in_r1, -s1, jnp.where(in_r2, s2, 0.0))
        for sl in range(N_SLABS):
            xp_refs[0][:, sl * LANES:(sl + 1) * LANES] = x_slabs[sl][0].astype(BF16)
        for gi in (1, 2):
            for blk, (start, stride) in enumerate(_stream_blocks(ATTN_DILATIONS[gi])):
                rows = pl.ds(start, ATTN_BLOCK, stride=stride)
                dst = slice(blk * ATTN_BLOCK, (blk + 1) * ATTN_BLOCK)
                cos_ref[gi, dst, :] = cos_ref[0, rows, :]
                sin_ref[gi, dst, :] = sin_ref[0, rows, :]

    t = c % 3
    scale = jnp.where(t == 0, HEAD_DIM ** -0.5, 1.0).astype(F32)
    rotate = t < 2

    def project(gi):
        def permute_next(i):
            stride = ATTN_DILATIONS[gi + 1]
            blk = jnp.minimum(t * PERM_BLOCKS_PER_STEP + i, ATTN_TILE // ATTN_BLOCK - 1)
            if stride == 4:
                start = (blk >> 2) * (4 * ATTN_BLOCK) + (blk & 3)
            else:
                start = blk
            dst = pl.ds(pl.multiple_of(blk * ATTN_BLOCK, ATTN_BLOCK), ATTN_BLOCK)
            for sl in range(N_SLABS):
                xp_refs[gi + 1][dst, sl * LANES:(sl + 1) * LANES] = (
                    x_slabs[sl][0, pl.ds(start, ATTN_BLOCK, stride=stride), :].astype(BF16))

        n_rc = len(QKV_ROW_CHUNKS)
        for rc, n_rows in enumerate(QKV_ROW_CHUNKS):
            rows = slice(sum(QKV_ROW_CHUNKS[:rc]), sum(QKV_ROW_CHUNKS[:rc + 1]))
            if gi + 1 < len(ATTN_DILATIONS):
                for i in range(rc, PERM_BLOCKS_PER_STEP, n_rc):
                    permute_next(i)
            y = jnp.dot(xp_refs[gi][rows, :], w_ref[...], preferred_element_type=F32)
            cos = jnp.where(rotate, cos_ref[gi, rows, :], 1.0) * scale
            sin = jnp.where(rotate, sin_ref[gi, rows, :], 0.0) * scale
            for hd in range(HEADS):
                yh = y[:, hd * HEAD_DIM:(hd + 1) * HEAD_DIM]
                o_ref[0, 0, hd, rows, :] = (yh * cos + pltpu.roll(yh, R2_LANE, axis=1) * sin).astype(BF16)

    for gi in range(len(ATTN_DILATIONS)):
        pl.when(c // 3 == gi)(functools.partial(project, gi))


def _qkv_proj(x, posc, freq, w_qkv, layer):
    b, s, _ = x.shape
    n_chunks = w_qkv.shape[1]
    slab = lambda sl: pl.BlockSpec((1, ATTN_TILE, LANES), lambda bi, ti, c: (bi, ti, sl))
    return pl.pallas_call(
        _qkv_kernel,
        grid=(b, s // ATTN_TILE, n_chunks),
        in_specs=[slab(sl) for sl in range(N_SLABS)] + [
                  pl.BlockSpec((1, ATTN_TILE // SUBLANES, LANES), lambda bi, ti, c: (bi, ti, 0)),
                  pl.BlockSpec((1, LANES), lambda bi, ti, c: (0, 0)),
                  pl.BlockSpec((None, None, D_MODEL, D_MODEL), lambda bi, ti, c: (layer, c, 0, 0))],
        out_specs=pl.BlockSpec((1, 1, HEADS, ATTN_TILE, HEAD_DIM), lambda bi, ti, c: (c, bi, 0, ti, 0)),
        out_shape=jax.ShapeDtypeStruct((n_chunks, b, HEADS, s, HEAD_DIM), BF16),
        scratch_shapes=[pltpu.VMEM((ATTN_TILE, D_MODEL), BF16)] * 3 + [
                        pltpu.VMEM((3, ATTN_TILE, LANES), F32),
                        pltpu.VMEM((3, ATTN_TILE, LANES), F32)],
        compiler_params=pltpu.CompilerParams(
            dimension_semantics=("arbitrary", "arbitrary", "arbitrary"),
            vmem_limit_bytes=VMEM_LIMIT_BYTES),
        name="qkv_proj",
    )(*([x] * N_SLABS), posc, freq, w_qkv)


def _attn_kernel(q0, k0, v0, q1, k1, v1, q2, k2, v2,
                 pk0, pv0, pk1, pv1, pk2, pv2, o_ref, acc_ref, m_ref, l_ref):
    has_prev = pl.program_id(1) > 0
    qi = lax.broadcasted_iota(jnp.int32, (ATTN_BLOCK, 2 * ATTN_BLOCK), 0)
    kj = lax.broadcasted_iota(jnp.int32, (ATTN_BLOCK, 2 * ATTN_BLOCK), 1)
    band = (kj >= qi) & (kj <= qi + ATTN_BLOCK)
    band_first = band & ((kj >= ATTN_BLOCK) | has_prev)
    ones = jnp.ones((2 * ATTN_BLOCK, LANES), BF16)
    qs, ks, vs = (q0, q1, q2), (k0, k1, k2), (v0, v1, v2)
    pks, pvs = (pk0, pk1, pk2), (pv0, pv1, pv2)
    for hh in range(ATTN_HEADS_PER_STEP):
        blk_rows = lambda r, i, hh=hh: r[0, 0, hh, i * ATTN_BLOCK:(i + 1) * ATTN_BLOCK, :]
        g0 = hh * len(ATTN_DILATIONS)
        _attn_head(qs, ks, vs, pks, pvs, blk_rows, band, band_first, ones, acc_ref, m_ref, l_ref, g0)
        _merge_groups(acc_ref, m_ref, l_ref, g0, o_ref, hh)


def _merge_groups(acc_ref, m_ref, l_ref, g0, o_ref, hh):
    for c in range(ATTN_TILE // ATTN_BLOCK):
        rows = slice(c * ATTN_BLOCK, (c + 1) * ATTN_BLOCK)
        m0, m1, m2 = m_ref[g0, rows, :], m_ref[g0 + 1, rows, :], m_ref[g0 + 2, rows, :]
        mx = jnp.maximum(jnp.maximum(m0, m1), m2)
        e0, e1, e2 = jnp.exp(m0 - mx), jnp.exp(m1 - mx), jnp.exp(m2 - mx)
        num = e0 * acc_ref[g0, rows, :] + e1 * acc_ref[g0 + 1, rows, :] + e2 * acc_ref[g0 + 2, rows, :]
        den = e0 * l_ref[g0, rows, :] + e1 * l_ref[g0 + 1, rows, :] + e2 * l_ref[g0 + 2, rows, :]
        o_ref[0, rows, hh * HEAD_DIM:(hh + 1) * HEAD_DIM] = (num / den).astype(BF16)


def _attn_head(qs, ks, vs, pks, pvs, blk_rows, band, band_first, ones, acc_ref, m_ref, l_ref, g0):
    contract_last = (((1,), (1,)), ((), ()))
    for gi, dilation in enumerate(ATTN_DILATIONS):
        n_streams = {1: 1, 4: 4, 16: 16}[dilation]
        for blk, (start, stride) in enumerate(_stream_blocks(dilation)):
            if blk < n_streams:
                k_prev, v_prev = blk_rows(pks[gi], blk), blk_rows(pvs[gi], blk)
            else:
                k_prev, v_prev = blk_rows(ks[gi], blk - n_streams), blk_rows(vs[gi], blk - n_streams)
            kcat = jnp.concatenate([k_prev, blk_rows(ks[gi], blk)], axis=0)
            vext = jnp.concatenate(
                [jnp.concatenate([v_prev, blk_rows(vs[gi], blk)], axis=0), ones], axis=1)
            s = lax.dot_general(blk_rows(qs[gi], blk), kcat, contract_last,
                                preferred_element_type=F32)
            s = jnp.where(band_first if blk < n_streams else band, s, NEG_BIG)
            m = jnp.max(s, axis=-1, keepdims=True)
            p = jnp.exp(s - m)
            pvl = jnp.dot(p.astype(BF16), vext, preferred_element_type=F32)
            rows = pl.ds(start, ATTN_BLOCK, stride=stride) if stride > 1 else pl.ds(start, ATTN_BLOCK)
            acc_ref[g0 + gi, rows, :] = pvl[:, :HEAD_DIM]
            l_ref[g0 + gi, rows, :] = pvl[:, HEAD_DIM:]
            m_ref[g0 + gi, rows, :] = jnp.broadcast_to(m, (ATTN_BLOCK, LANES))


def _attention(qkv):
    _, b, _, s, _ = qkv.shape
    n_tiles = s // ATTN_TILE

    hps = ATTN_HEADS_PER_STEP

    def cur(chunk):
        return pl.BlockSpec((1, 1, hps, ATTN_TILE, HEAD_DIM), lambda bi, ti, hd: (chunk, bi, hd, ti, 0))

    def prev(chunk, rows):
        per_tile = ATTN_TILE // rows
        return pl.BlockSpec(
            (1, 1, hps, rows, HEAD_DIM),
            lambda bi, ti, hd: (chunk, bi, hd, jnp.maximum(ti * per_tile - 1, 0), 0))

    prev_rows = (ATTN_BLOCK, 4 * ATTN_BLOCK, ATTN_TILE)
    in_specs = [cur(c) for c in range(9)]
    operands = [qkv] * 9
    for gi in range(3):
        in_specs += [prev(3 * gi + 1, prev_rows[gi]), prev(3 * gi + 2, prev_rows[gi])]
        operands += [qkv, qkv]
    return pl.pallas_call(
        _attn_kernel,
        grid=(b, n_tiles, HEADS // hps),
        in_specs=in_specs,
        out_specs=pl.BlockSpec((1, ATTN_TILE, hps * HEAD_DIM), lambda bi, ti, hd: (bi, ti, hd)),
        out_shape=jax.ShapeDtypeStruct((b, s, HEADS * HEAD_DIM), BF16),
        scratch_shapes=[pltpu.VMEM((hps * len(ATTN_DILATIONS), ATTN_TILE, LANES), F32)] * 3,
        compiler_params=pltpu.CompilerParams(
            dimension_semantics=("arbitrary", "arbitrary", "arbitrary"),
            vmem_limit_bytes=VMEM_LIMIT_BYTES),
        name="attention",
    )(*operands)


def _qkv_weight_kernel(w_ref, o_ref):
    half = ROT_DIM // 2
    is_qk = pl.program_id(1) % 3 < 2
    lane = lax.broadcasted_iota(jnp.int32, (D_MODEL, HEAD_DIM), 1)
    keep = jnp.logical_not(is_qk) | (lane < half) | (lane >= R2_LANE + half)
    for hd in range(HEADS):
        w = w_ref[:, hd * HEAD_DIM:(hd + 1) * HEAD_DIM]
        moved = jnp.where(lane < R2_LANE, pltpu.roll(w, HEAD_DIM - half, axis=1),
                          pltpu.roll(w, R2_LANE - half, axis=1))
        o_ref[:, hd * HEAD_DIM:(hd + 1) * HEAD_DIM] = jnp.where(keep, w, moved).astype(BF16)


def _qkv_weights(w_qkv):
    n_layers, d, n_cols = w_qkv.shape
    n_chunks = n_cols // D_MODEL
    return pl.pallas_call(
        _qkv_weight_kernel,
        grid=(n_layers, n_chunks),
        in_specs=[pl.BlockSpec((None, d, D_MODEL), lambda li, c: (li, 0, c))],
        out_specs=pl.BlockSpec((None, None, d, D_MODEL), lambda li, c: (li, c, 0, 0)),
        out_shape=jax.ShapeDtypeStruct((n_layers, n_chunks, d, D_MODEL), BF16),
        compiler_params=pltpu.CompilerParams(
            dimension_semantics=("arbitrary", "arbitrary"), vmem_limit_bytes=VMEM_LIMIT_BYTES),
        name="qkv_weights",
    )(w_qkv)


def kernel(x, positions, ev_w_in, ev_conv_w, ev_conv_b, ev_conv_ln_g, ev_conv_ln_b,
           ev_sgu_ln_g, ev_sgu_ln_b, ev_w_spatial, ev_b_spatial, ev_w_out,
           od_w_qkv, od_w_out, ffn_w_gate, ffn_w_up, ffn_w_down, ln_g, ln_b):
    b, s, d = x.shape
    assert (d, s % ATTN_TILE, (b * s) % TM_BACK) == (D_MODEL, 0, 0)
    posc = jnp.repeat(positions.reshape(b, s // SUBLANES, SUBLANES), LANES // SUBLANES, axis=2)
    inv_freq = ROPE_THETA ** (-jnp.arange(0, ROT_DIM, 2, dtype=F32) / ROT_DIM)
    freq = jnp.tile(inv_freq, LANES // (ROT_DIM // 2)).reshape(1, LANES).astype(F32)
    w_in, w_qkv = ev_w_in.astype(BF16), _qkv_weights(od_w_qkv)
    w_out = (ev_w_out.astype(BF16), od_w_out.astype(BF16))
    w_gate, w_up, w_down = ffn_w_gate.astype(BF16), ffn_w_up.astype(BF16), ffn_w_down.astype(BF16)

    for layer in range(DEPTH):
        i = layer // 2
        if layer % 2 == 0:
            cvec = jnp.stack([ev_conv_b[i], ev_conv_ln_g[i], ev_conv_ln_b[i]])
            svec = jnp.stack([ev_sgu_ln_g[i], ev_sgu_ln_b[i]])
            bias_full = jnp.repeat(ev_b_spatial[i].T, SGU_CH // SGU_GROUPS, axis=1)
            c = _even_front(x, w_in, i, ev_conv_w[i], cvec, svec, ev_w_spatial[i], bias_full)
        else:
            c = _attention(_qkv_proj(x, posc, freq, w_qkv, i))
        ln4 = jnp.stack([ln_g[layer, 0], ln_b[layer, 0], ln_g[layer, 1], ln_b[layer, 1]])
        x = _back(x.reshape(b * s, d), c.reshape(b * s, d), w_out[layer % 2], i,
                  w_gate, w_up, w_down, layer, ln4).reshape(b, s, d)
    return x
```
